```python
import jax, jax.numpy as jnp
from jax import lax
import numpy as np

D_MODEL = 4096
BATCH = 1
SEQ = 16384
DEPTH = 2

FOX_HEAD_DIM = 128
FOX_WIDTH = D_MODEL // 2
FOX_HEADS = FOX_WIDTH // FOX_HEAD_DIM
FOX_BLOCK = 128
GLA_HEADS = 4
GLA_VAL_WIDTH = D_MODEL - FOX_WIDTH
GLA_DV = GLA_VAL_WIDTH // GLA_HEADS
GLA_DK = GLA_DV // 2
GLA_KEY_WIDTH = GLA_DK * GLA_HEADS
GLA_GATE_RANK = 16
GLA_TAU = 16.0
GLA_CHUNK = 64
MIX_WIDTH = FOX_WIDTH + GLA_VAL_WIDTH
IN_SIZES = (FOX_WIDTH, FOX_WIDTH, FOX_WIDTH, FOX_HEADS,
            GLA_KEY_WIDTH, GLA_KEY_WIDTH, GLA_VAL_WIDTH, GLA_VAL_WIDTH, GLA_GATE_RANK)
IN_WIDTH = sum(IN_SIZES)
N_EXPERTS = 32
TOP_K = 4
D_FF_EXPERT = D_MODEL // 8
SWIGLU_LIMIT = 7.0
SWIGLU_ALPHA = 1.702
PLE_DIM = 256
DEEPNORM_ALPHA = (2 * DEPTH) ** 0.25
DEEPNORM_BETA = (8 * DEPTH) ** -0.25
LN_EPS = 1e-5
RMS_EPS = 1e-6

kernel_name = 'hybrid_fox_gla_moe_deepnorm'


def _split_points():
    return [int(v) for v in np.cumsum(IN_SIZES)[:-1]]


def layer_norm(x, g, b):
    xf = x.astype(jnp.float32)
    mu = jnp.mean(xf, axis=-1, keepdims=True)
    xc = xf - mu
    var = jnp.mean(xc * xc, axis=-1, keepdims=True)
    return (xc * lax.rsqrt(var + LN_EPS) * g + b).astype(x.dtype)


def forgetting_attention(q, k, v, f_logit):
    B, S, H, Dh = q.shape
    nb = S // FOX_BLOCK
    scale = Dh ** -0.5
    c = jnp.cumsum(jax.nn.log_sigmoid(f_logit.astype(jnp.float32)), axis=1)
    ck = c.transpose(0, 2, 1)[:, :, None, :]
    qb = q.reshape(B, nb, FOX_BLOCK, H, Dh).transpose(1, 0, 2, 3, 4)
    cb = c.reshape(B, nb, FOX_BLOCK, H).transpose(1, 0, 2, 3)
    starts = jnp.arange(nb) * FOX_BLOCK
    k_pos = jnp.arange(S)

    def block(args):
        qi, ci, s0 = args
        sc = jnp.einsum('bqhd,bkhd->bhqk', qi, k).astype(jnp.float32) * scale
        sc = sc + (ci.transpose(0, 2, 1)[..., None] - ck)
        q_pos = s0 + jnp.arange(FOX_BLOCK)
        sc = jnp.where(k_pos[None, :] <= q_pos[:, None], sc, -jnp.inf)
        pr = jax.nn.softmax(sc, axis=-1)
        return jnp.einsum('bhqk,bkhd->bqhd', pr.astype(v.dtype), v)

    out = lax.map(block, (qb, cb, starts))
    return out.transpose(1, 0, 2, 3, 4).reshape(B, S, H, Dh)


def gated_linear_attention(q, k, v, log_alpha):
    B, S, H, DK = q.shape
    DV = v.shape[-1]
    n = S // GLA_CHUNK

    def chunks(t):
        return t.astype(jnp.float32).reshape(B, n, GLA_CHUNK, H, t.shape[-1]).transpose(1, 0, 2, 3, 4)

    causal = jnp.tril(jnp.ones((GLA_CHUNK, GLA_CHUNK), dtype=bool))[None, :, :, None, None]

    def step(state, inp):
        qc, kc, vc, gc = inp
        b = jnp.cumsum(gc, axis=1)
        o_inter = jnp.einsum('bihk,bhkv->bihv', qc * jnp.exp(b), state)
        decay = jnp.exp(jnp.where(causal, b[:, :, None] - b[:, None], -jnp.inf))
        scores = jnp.einsum('bihk,bjhk,bijhk->bhij', qc, kc, decay)
        o_intra = jnp.einsum('bhij,bjhv->bihv', scores, vc)
        b_last = b[:, -1]
        k_dec = kc * jnp.exp(b_last[:, None] - b)
        state = state * jnp.exp(b_last)[..., None] + jnp.einsum('bjhk,bjhv->bhkv', k_dec, vc)
        return state, o_inter + o_intra

    s0 = jnp.zeros((B, H, DK, DV), jnp.float32)
    _, o = lax.scan(step, s0, (chunks(q * DK ** -0.5), chunks(k), chunks(v), chunks(log_alpha)))
    return o.transpose(1, 0, 2, 3, 4).reshape(B, S, H, DV)


def hybrid_mixer(h, w_in, b_forget, w_gla_gate, b_gla_gate, g_gla_norm, w_out):
    B, S, _ = h.shape
    z = jnp.einsum('bsd,dn->bsn', h, w_in)
    fq, fk, fv, ff, gq, gk, gv, gr, glr = jnp.split(z, _split_points(), axis=-1)
    fox = forgetting_attention(fq.reshape(B, S, FOX_HEADS, FOX_HEAD_DIM),
                               fk.reshape(B, S, FOX_HEADS, FOX_HEAD_DIM),
                               fv.reshape(B, S, FOX_HEADS, FOX_HEAD_DIM),
                               ff + b_forget)
    log_alpha = jax.nn.log_sigmoid((glr @ w_gla_gate + b_gla_gate).astype(jnp.float32)) / GLA_TAU
    o = gated_linear_attention(gq.reshape(B, S, GLA_HEADS, GLA_DK),
                               gk.reshape(B, S, GLA_HEADS, GLA_DK),
                               gv.reshape(B, S, GLA_HEADS, GLA_DV),
                               log_alpha.reshape(B, S, GLA_HEADS, GLA_DK))
    o = o * lax.rsqrt(jnp.mean(o * o, axis=-1, keepdims=True) + RMS_EPS) * g_gla_norm
    gla = (o.reshape(B, S, GLA_VAL_WIDTH) * jax.nn.silu(gr.astype(jnp.float32))).astype(h.dtype)
    mixed = jnp.concatenate([fox.reshape(B, S, FOX_WIDTH).astype(h.dtype), gla], axis=-1)
    return jnp.einsum('bsm,md->bsd', mixed, w_out)


def expert_ffn(h, w_router, b_router, w_gate_up, b_gate_up, w_down, b_down):
    B, S, D = h.shape
    t = h.reshape(B * S, D)
    logits = (t @ w_router + b_router).astype(jnp.float32)
    top_v, top_i = lax.top_k(logits, TOP_K)
    gates = jax.nn.softmax(top_v, axis=-1)
    combine = jnp.einsum('tk,tke->te', gates, jax.nn.one_hot(top_i, N_EXPERTS, dtype=jnp.float32))
    combine = combine.astype(t.dtype)
    out = jnp.zeros_like(t)
    for e in range(N_EXPERTS):
        gu = t @ w_gate_up[e] + b_gate_up[e]
        g = jnp.minimum(gu[:, :D_FF_EXPERT], SWIGLU_LIMIT)
        u = jnp.clip(gu[:, D_FF_EXPERT:], -SWIGLU_LIMIT, SWIGLU_LIMIT)
        act = (u + 1.0) * (g * jax.nn.sigmoid(SWIGLU_ALPHA * g))
        out = out + combine[:, e:e + 1] * (act @ w_down[e] + b_down[e])
    return out.reshape(B, S, D)


def setup_inputs(seed: int = 0) -> dict:
    key = jax.random.key(seed)
    ks = jax.random.split(key, 22)
    L, D, E, F = DEPTH, D_MODEL, N_EXPERTS, D_FF_EXPERT

    def nrm(k, shape, scale):
        return jax.random.normal(k, shape, jnp.float32) * scale

    return {
        'x': nrm(ks[0], (BATCH, SEQ, D), 1.0),
        'p': nrm(ks[1], (DEPTH, BATCH, SEQ, PLE_DIM), 1.0),
        'w_in': nrm(ks[2], (L, D, IN_WIDTH), D ** -0.5),
        'b_forget': nrm(ks[3], (L, FOX_HEADS), 0.1),
        'w_gla_gate': nrm(ks[4], (L, GLA_GATE_RANK, GLA_KEY_WIDTH), GLA_GATE_RANK ** -0.5),
        'b_gla_gate': nrm(ks[5], (L, GLA_KEY_WIDTH), 0.1),
        'g_gla_norm': 1.0 + nrm(ks[6], (L, GLA_DV), 0.1),
        'w_out': nrm(ks[7], (L, MIX_WIDTH, D), MIX_WIDTH ** -0.5 * DEEPNORM_BETA),
        'g_ln1': 1.0 + nrm(ks[8], (L, D), 0.1),
        'b_ln1': nrm(ks[9], (L, D), 0.02),
        'w_router': nrm(ks[10], (L, D, E), D ** -0.5),
        'b_router': nrm(ks[11], (L, E), 0.01),
        'w_gate_up': nrm(ks[12], (L, E, D, 2 * F), D ** -0.5),
        'b_gate_up': nrm(ks[13], (L, E, 2 * F), 0.02),
        'w_down': nrm(ks[14], (L, E, F, D), F ** -0.5 * DEEPNORM_BETA),
        'b_down': nrm(ks[15], (L, E, D), 0.01),
        'g_ln2': 1.0 + nrm(ks[16], (L, D), 0.1),
        'b_ln2': nrm(ks[17], (L, D), 0.02),
        'w_ple_gate': nrm(ks[18], (L, D, D), D ** -0.5),
        'w_ple_proj': nrm(ks[19], (L, PLE_DIM, D), PLE_DIM ** -0.5 * DEEPNORM_BETA),
        'g_ln3': 1.0 + nrm(ks[20], (L, D), 0.1),
        'b_ln3': nrm(ks[21], (L, D), 0.02),
    }


def reference(x, p, w_in, b_forget, w_gla_gate, b_gla_gate, g_gla_norm, w_out,
              g_ln1, b_ln1, w_router, b_router, w_gate_up, b_gate_up, w_down, b_down,
              g_ln2, b_ln2, w_ple_gate, w_ple_proj, g_ln3, b_ln3):
    h = x
    for i in range(DEPTH):
        m = hybrid_mixer(h, w_in[i], b_forget[i], w_gla_gate[i], b_gla_gate[i], g_gla_norm[i], w_out[i])
        h = layer_norm(DEEPNORM_ALPHA * h + m, g_ln1[i], b_ln1[i])
        f = expert_ffn(h, w_router[i], b_router[i], w_gate_up[i], b_gate_up[i], w_down[i], b_down[i])
        h = layer_norm(DEEPNORM_ALPHA * h + f, g_ln2[i], b_ln2[i])
        e = jnp.einsum('bsr,rd->bsd', p[i], w_ple_proj[i])
        gate = jax.nn.sigmoid(jnp.einsum('bsd,de->bse', h, w_ple_gate[i]))
        h = layer_norm(DEEPNORM_ALPHA * h + gate * e, g_ln3[i], b_ln3[i])
    return h
```

```python
import functools

import jax
import jax.numpy as jnp
import numpy as np
from jax import lax
from jax.experimental import pallas as pl
from jax.experimental.pallas import tpu as pltpu

TOP_K = 4
GLA_TAU = 16.0
SWIGLU_LIMIT = 7.0
SWIGLU_ALPHA = 1.702
LN_EPS = 1e-5
RMS_EPS = 1e-6

LANES = 128
VMEM_LIMIT_BYTES = 56 * 1024 * 1024
NEG_BIG = -1e30

GLA_CHUNK = 64
GLA_SUB = 16

F32 = jnp.float32
BF16 = jnp.bfloat16


def _cparams(sem):
    return pltpu.CompilerParams(dimension_semantics=sem, vmem_limit_bytes=VMEM_LIMIT_BYTES)


def _split_bf16(x):
    hi = x.astype(BF16)
    lo = (x - hi.astype(F32)).astype(BF16)
    return hi, lo


def _log_sigmoid(v):
    return jnp.minimum(v, 0.0) - jnp.log(1.0 + jnp.exp(-jnp.abs(v)))


def _sigmoid(v):
    return 1.0 / (1.0 + jnp.exp(-v))


def _dot(a, b):
    return jnp.dot(a, b, preferred_element_type=F32)


def _dot_nt(a, b):
    return lax.dot_general(a, b, (((1,), (1,)), ((), ())), preferred_element_type=F32)


def _dot_tn(a, b):
    return lax.dot_general(a, b, (((0,), (0,)), ((), ())), preferred_element_type=F32)


def _layer_norm_rows(x, g, b):
    mu = jnp.mean(x, axis=-1, keepdims=True)
    xc = x - mu
    var = jnp.mean(xc * xc, axis=-1, keepdims=True)
    return xc * lax.rsqrt(var + LN_EPS) * g + b


def _inproj_kernel(x_ref, w_ref, o_ref):
    acc = _dot(x_ref[...], w_ref[...])
    for c in range(o_ref.shape[0]):
        o_ref[c] = acc[:, c * LANES:(c + 1) * LANES].astype(o_ref.dtype)


def _inproj(xb, w, tm, tn):
    t, d = xb.shape
    n = w.shape[1]
    nc = tn // LANES
    return pl.pallas_call(
        _inproj_kernel,
        out_shape=jax.ShapeDtypeStruct((n // LANES, t, LANES), BF16),
        grid=(t // tm, n // tn),
        in_specs=[pl.BlockSpec((tm, d), lambda i, j: (i, 0)),
                  pl.BlockSpec((d, tn), lambda i, j: (0, j))],
        out_specs=pl.BlockSpec((nc, tm, LANES), lambda i, j: (j, i, 0)),
        compiler_params=_cparams(("parallel", "arbitrary")),
        name="inproj",
    )(xb, w)


def _aux_kernel(x_ref, whi_ref, wlo_ref, bias_ref, aux_ref, cl_ref, off_ref, carry_ref):
    i = pl.program_id(0)

    @pl.when(i == 0)
    def _():
        carry_ref[...] = jnp.zeros_like(carry_ref)

    x = x_ref[...]
    acc = _dot(x, whi_ref[...]) + _dot(x, wlo_ref[...])
    aux_ref[...] = acc
    ls = _log_sigmoid(acc + bias_ref[...])
    tm = x.shape[0]
    r = lax.broadcasted_iota(jnp.int32, (tm, tm), 0)
    c = lax.broadcasted_iota(jnp.int32, (tm, tm), 1)
    tri = (c <= r).astype(BF16)
    hi, lo = _split_bf16(ls)
    cl = _dot(tri, hi) + _dot(tri, lo)
    cl_ref[...] = cl
    off_ref[0] = carry_ref[...]
    carry_ref[...] = carry_ref[...] + cl[tm - 1:tm, :]


def _aux_proj(xb, whi, wlo, bias, tm):
    t, d = xb.shape
    nt = t // tm
    return pl.pallas_call(
        _aux_kernel,
        out_shape=(jax.ShapeDtypeStruct((t, LANES), F32),
                   jax.ShapeDtypeStruct((t, LANES), F32),
                   jax.ShapeDtypeStruct((nt, 1, LANES), F32)),
        grid=(nt,),
        in_specs=[pl.BlockSpec((tm, d), lambda i: (i, 0)),
                  pl.BlockSpec((d, LANES), lambda i: (0, 0)),
                  pl.BlockSpec((d, LANES), lambda i: (0, 0)),
                  pl.BlockSpec((1, LANES), lambda i: (0, 0))],
        out_specs=(pl.BlockSpec((tm, LANES), lambda i: (i, 0)),
                   pl.BlockSpec((tm, LANES), lambda i: (i, 0)),
                   pl.BlockSpec((1, 1, LANES), lambda i: (i, 0, 0))),
        scratch_shapes=[pltpu.VMEM((1, LANES), F32)],
        compiler_params=_cparams(("arbitrary",)),
        name="aux_proj",
    )(xb, whi, wlo, bias)


def _fox_kernel(q_ref, k_ref, v_ref, cl_ref, off_ref, o_ref, m_ref, l_ref, acc_ref, *, scale):
    i = pl.program_id(1)
    tq = q_ref.shape[1]
    q = (q_ref[0].astype(F32) * scale).astype(BF16)
    off_i = off_ref[0, pl.ds(i, 1), :]

    m_ref[...] = jnp.full_like(m_ref, NEG_BIG)
    l_ref[...] = jnp.zeros_like(l_ref)
    acc_ref[...] = jnp.zeros_like(acc_ref)

    def step(j, masked):
        start = pl.multiple_of(j * tq, tq)
        k = k_ref[0, pl.ds(start, tq), :]
        v = v_ref[0, pl.ds(start, tq), :]
        bias = -cl_ref[0, pl.ds(j, 1), :] - (off_ref[0, pl.ds(j, 1), :] - off_i)
        s = _dot_nt(q, k) + bias
        if masked:
            r = lax.broadcasted_iota(jnp.int32, (tq, tq), 0)
            c = lax.broadcasted_iota(jnp.int32, (tq, tq), 1)
            s = jnp.where(c <= r, s, NEG_BIG)
        m_old = m_ref[...]
        m_new = jnp.maximum(m_old, jnp.max(s, axis=1, keepdims=True))
        alpha = jnp.exp(m_old - m_new)
        p = jnp.exp(s - m_new)
        l_ref[...] = alpha * l_ref[...] + jnp.sum(p, axis=1, keepdims=True)
        acc_ref[...] = alpha * acc_ref[...] + _dot(p.astype(BF16), v)
        m_ref[...] = m_new

    def body(j, carry):
        step(j, False)
        return carry

    lax.fori_loop(0, i, body, 0)
    step(i, True)
    o_ref[...] = (acc_ref[...] / l_ref[...]).astype(o_ref.dtype)


def _fox(z3, cl3, off3, heads, tq):
    _, t, dh = z3.shape
    nq = t // tq
    kern = functools.partial(_fox_kernel, scale=float(dh) ** -0.5)
    return pl.pallas_call(
        kern,
        out_shape=jax.ShapeDtypeStruct((t, heads * dh), BF16),
        grid=(heads, nq),
        in_specs=[pl.BlockSpec((1, tq, dh), lambda h, i: (h, i, 0)),
                  pl.BlockSpec((1, t, dh), lambda h, i: (heads + h, 0, 0)),
                  pl.BlockSpec((1, t, dh), lambda h, i: (2 * heads + h, 0, 0)),
                  pl.BlockSpec((1, nq, tq), lambda h, i: (h, 0, 0)),
                  pl.BlockSpec((1, nq, tq), lambda h, i: (h, 0, 0))],
        out_specs=pl.BlockSpec((tq, dh), lambda h, i: (i, h)),
        scratch_shapes=[pltpu.VMEM((tq, 1), F32), pltpu.VMEM((tq, 1), F32),
                        pltpu.VMEM((tq, dh), F32)],
        compiler_params=_cparams(("parallel", "arbitrary")),
        name="fox_attention",
    )(z3, z3, z3, cl3, off3)


def _gla_kernel(q_ref, k_ref, v_ref, r_ref, aux_ref, wg_hi_ref, wg_lo_ref, bg_ref, gn_ref,
                o_ref, st_ref, b_ref, *, heads, dk, dv):
    step = pl.program_id(0)
    ch = GLA_CHUNK
    sb = GLA_SUB
    nsb = ch // sb
    kc = dk // LANES
    vc = dv // LANES

    @pl.when(step == 0)
    def _():
        st_ref[...] = jnp.zeros_like(st_ref)

    a_hi, a_lo = _split_bf16(aux_ref[...])
    r = lax.broadcasted_iota(jnp.int32, (ch, ch), 0)
    c = lax.broadcasted_iota(jnp.int32, (ch, ch), 1)
    tri = (c <= r).astype(BF16)
    for h in range(heads):
        glog = (_dot(a_hi, wg_hi_ref[h]) + _dot(a_lo, wg_hi_ref[h]) + _dot(a_hi, wg_lo_ref[h])
                + bg_ref[h])
        g = _log_sigmoid(glog) * (1.0 / GLA_TAU)
        g_hi, g_lo = _split_bf16(g)
        b_ref[h] = _dot(tri, g_hi) + _dot(tri, g_lo)

    row = lax.broadcasted_iota(jnp.int32, (ch, 1), 0)
    jrow = lax.broadcasted_iota(jnp.int32, (sb, 1), 0)
    lane = lax.broadcasted_iota(jnp.int32, (sb, LANES), 1)
    qscale = float(dk) ** -0.5

    def head_body(h, carry):
        b = b_ref[h]
        q = jnp.concatenate([q_ref[h * kc + u] for u in range(kc)], axis=1).astype(F32) * qscale
        k = jnp.concatenate([k_ref[h * kc + u] for u in range(kc)], axis=1).astype(F32)
        v = jnp.concatenate([v_ref[h * vc + u] for u in range(vc)], axis=1)
        gr = jnp.concatenate([r_ref[h * vc + u] for u in range(vc)], axis=1).astype(F32)
        st = st_ref[h]

        qe = (q * jnp.exp(b)).astype(BF16)
        o = _dot_nt(qe, st.astype(BF16))
        bl = b[ch - 1:ch, :]
        kd = (k * jnp.exp(bl - b)).astype(BF16)
        st_ref[h] = st * jnp.exp(bl) + _dot_tn(v, kd)

        a_rows = [jnp.zeros((sb, ch), F32)]
        for bi in range(1, nsb):
            lo = bi * sb
            ref = b[lo - 1:lo, :]
            qt = (q[lo:lo + sb] * jnp.exp(b[lo:lo + sb] - ref)).astype(BF16)
            kt = (k * jnp.exp(jnp.where(row < lo, ref - b, NEG_BIG))).astype(BF16)
            a_rows.append(_dot_nt(qt, kt))
        a_off = jnp.concatenate(a_rows, axis=0).astype(BF16)
        o = o + _dot(a_off, v)

        o_diag = []
        for bi in range(nsb):
            lo = bi * sb
            bj = b[lo:lo + sb]
            kj = k[lo:lo + sb]
            at = jnp.zeros((sb, LANES), F32)
            for il in range(sb):
                i = lo + il
                e = jnp.where(jrow <= il, b[i:i + 1] - bj, NEG_BIG)
                tt = kj * jnp.exp(e) * q[i:i + 1]
                a = jnp.sum(tt, axis=1, keepdims=True)
                at = jnp.where(lane == il, a, at)
            od = _dot_tn(at.astype(BF16), v[lo:lo + sb])
            o_diag.append(od[0:sb])
        o = o + jnp.concatenate(o_diag, axis=0)

        ms = jnp.mean(o * o, axis=1, keepdims=True)
        on = o * lax.rsqrt(ms + RMS_EPS) * gn_ref[...]
        o_ref[h] = (on * (gr * _sigmoid(gr))).astype(o_ref.dtype)
        return carry

    lax.fori_loop(0, heads, head_body, 0)


def _gla(z3, aux, wg_hi, wg_lo, bg, gn, cfg):
    heads, dk, dv = cfg["gh"], cfg["dk"], cfg["dv"]
    t = z3.shape[1]
    ch = GLA_CHUNK
    kb = heads * dk // LANES
    vb = heads * dv // LANES
    q0, k0, v0, r0 = cfg["gq_blk"], cfg["gk_blk"], cfg["gv_blk"], cfg["gr_blk"]
    assert q0 % kb == 0 and k0 % kb == 0 and v0 % vb == 0 and r0 % vb == 0
    kern = functools.partial(_gla_kernel, heads=heads, dk=dk, dv=dv)
    return pl.pallas_call(
        kern,
        out_shape=jax.ShapeDtypeStruct((heads, t, dv), BF16),
        grid=(t // ch,),
        in_specs=[pl.BlockSpec((kb, ch, LANES), lambda i: (q0 // kb, i, 0)),
                  pl.BlockSpec((kb, ch, LANES), lambda i: (k0 // kb, i, 0)),
                  pl.BlockSpec((vb, ch, LANES), lambda i: (v0 // vb, i, 0)),
                  pl.BlockSpec((vb, ch, LANES), lambda i: (r0 // vb, i, 0)),
                  pl.BlockSpec((ch, LANES), lambda i: (i, 0)),
                  pl.BlockSpec((heads, LANES, dk), lambda i: (0, 0, 0)),
                  pl.BlockSpec((heads, LANES, dk), lambda i: (0, 0, 0)),
                  pl.BlockSpec((heads, 1, dk), lambda i: (0, 0, 0)),
                  pl.BlockSpec((1, dv), lambda i: (0, 0))],
        out_specs=pl.BlockSpec((heads, ch, dv), lambda i: (0, i, 0)),
        scratch_shapes=[pltpu.VMEM((heads, dv, dk), F32), pltpu.VMEM((heads, ch, dk), F32)],
        compiler_params=_cparams(("arbitrary",)),
        name="gla",
    )(z3, z3, z3, z3, aux, wg_hi, wg_lo, bg, gn)


def _ln_epilogue(acc_ref, h_ref, g_ref, b_ref, of_ref, ob_ref, alpha, rows, extra=None):
    tm = acc_ref.shape[0]

    def body(r, carry):
        sl = pl.ds(pl.multiple_of(r * rows, rows), rows)
        y = acc_ref[sl, :]
        if extra is not None:
            y = extra(y, sl)
        hn = _layer_norm_rows(alpha * h_ref[sl, :] + y, g_ref[...], b_ref[...])
        of_ref[sl, :] = hn
        ob_ref[sl, :] = hn.astype(ob_ref.dtype)
        return carry

    lax.fori_loop(0, tm // rows, body, 0)


def _outproj_kernel(x1_ref, x2_ref, w_ref, h_ref, g_ref, b_ref, of_ref, ob_ref, *, k1, alpha):
    kk = pl.program_id(1)
    acc_ref = of_ref

    @pl.when(kk == 0)
    def _():
        acc_ref[...] = jnp.zeros_like(acc_ref)

    @pl.when(kk < k1)
    def _():
        acc_ref[...] += _dot(x1_ref[...], w_ref[...])

    @pl.when(kk >= k1)
    def _():
        acc_ref[...] += _dot(x2_ref[0], w_ref[...])

    @pl.when(kk == pl.num_programs(1) - 1)
    def _():
        _ln_epilogue(acc_ref, h_ref, g_ref, b_ref, of_ref, ob_ref, alpha, 64)


def _outproj_ln(fox, gla3, w, h, g, b, tm, alpha):
    t, fw = fox.shape
    gh, _, dv = gla3.shape
    d = w.shape[1]
    tk = dv
    assert fw % tk == 0
    k1 = fw // tk
    nk = k1 + gh
    kern = functools.partial(_outproj_kernel, k1=k1, alpha=alpha)
    return pl.pallas_call(
        kern,
        out_shape=(jax.ShapeDtypeStruct((t, d), F32), jax.ShapeDtypeStruct((t, d), BF16)),
        grid=(t // tm, nk),
        in_specs=[pl.BlockSpec((tm, tk), lambda i, k: (i, jnp.minimum(k, k1 - 1))),
                  pl.BlockSpec((1, tm, dv), lambda i, k: (jnp.maximum(k - k1, 0), i, 0)),
                  pl.BlockSpec((tk, d), lambda i, k: (k, 0)),
                  pl.BlockSpec((tm, d), lambda i, k: (i, 0)),
                  pl.BlockSpec((1, d), lambda i, k: (0, 0)),
                  pl.BlockSpec((1, d), lambda i, k: (0, 0))],
        out_specs=(pl.BlockSpec((tm, d), lambda i, k: (i, 0)),
                   pl.BlockSpec((tm, d), lambda i, k: (i, 0))),
        compiler_params=_cparams(("parallel", "arbitrary")),
        name="outproj_ln",
    )(fox, gla3, w, h, g, b)


def _ple_kernel(x_ref, w_ref, p_ref, wp_ref, h_ref, g_ref, b_ref, of_ref, ob_ref, *, alpha):
    kk = pl.program_id(1)
    acc_ref = of_ref

    @pl.when(kk == 0)
    def _():
        acc_ref[...] = jnp.zeros_like(acc_ref)

    acc_ref[...] += _dot(x_ref[...], w_ref[...])

    @pl.when(kk == pl.num_programs(1) - 1)
    def _():
        def gate(y, sl):
            e = _dot(p_ref[sl, :], wp_ref[...])
            return _sigmoid(y) * e

        _ln_epilogue(acc_ref, h_ref, g_ref, b_ref, of_ref, ob_ref, alpha, 64, extra=gate)


def _ple_ln(hb, w, pb, wp, h, g, b, tm, tk, alpha):
    t, d = h.shape
    r = pb.shape[1]
    kern = functools.partial(_ple_kernel, alpha=alpha)
    return pl.pallas_call(
        kern,
        out_shape=(jax.ShapeDtypeStruct((t, d), F32), jax.ShapeDtypeStruct((t, d), BF16)),
        grid=(t // tm, d // tk),
        in_specs=[pl.BlockSpec((tm, tk), lambda i, k: (i, k)),
                  pl.BlockSpec((tk, d), lambda i, k: (k, 0)),
                  pl.BlockSpec((tm, r), lambda i, k: (i, 0)),
                  pl.BlockSpec((r, d), lambda i, k: (0, 0)),
                  pl.BlockSpec((tm, d), lambda i, k: (i, 0)),
                  pl.BlockSpec((1, d), lambda i, k: (0, 0)),
                  pl.BlockSpec((1, d), lambda i, k: (0, 0))],
        out_specs=(pl.BlockSpec((tm, d), lambda i, k: (i, 0)),
                   pl.BlockSpec((tm, d), lambda i, k: (i, 0))),
        compiler_params=_cparams(("parallel", "arbitrary")),
        name="ple_ln",
    )(hb, w, pb, wp, h, g, b)


def _router_kernel(x_ref, whi_ref, wlo_ref, bias_ref, meta_ref, gate_ref, cnt_ref, carry_ref):
    i = pl.program_id(0)

    @pl.when(i == 0)
    def _():
        carry_ref[...] = jnp.zeros_like(carry_ref)

    x_hi, x_lo = _split_bf16(x_ref[...])
    logits = (_dot(x_hi, whi_ref[...]) + _dot(x_lo, whi_ref[...]) + _dot(x_hi, wlo_ref[...])
              + bias_ref[...])
    tm = logits.shape[0]
    lane = lax.broadcasted_iota(jnp.int32, (tm, LANES), 1).astype(F32)
    work = logits
    vals, idxs = [], []
    for _ in range(TOP_K):
        m = jnp.max(work, axis=1, keepdims=True)
        idx = jnp.min(jnp.where(work == m, lane, float(LANES)), axis=1, keepdims=True)
        vals.append(m)
        idxs.append(idx)
        work = jnp.where(lane == idx, 2.0 * NEG_BIG, work)
    exps = [jnp.exp(v - vals[0]) for v in vals]
    denom = exps[0] + exps[1] + exps[2] + exps[3]

    onehot = jnp.zeros((tm, LANES), F32)
    for idx in idxs:
        onehot = onehot + (lane == idx).astype(F32)
    r = lax.broadcasted_iota(jnp.int32, (tm, tm), 0)
    c = lax.broadcasted_iota(jnp.int32, (tm, tm), 1)
    strict = (c < r).astype(BF16)
    before = _dot(strict, onehot.astype(BF16)) + carry_ref[...]
    carry_ref[...] = carry_ref[...] + jnp.sum(onehot, axis=0, keepdims=True)
    cnt_ref[...] = carry_ref[...]

    meta = jnp.zeros((tm, LANES), jnp.int32)
    gates = jnp.zeros((tm, LANES), F32)
    for kq in range(TOP_K):
        rank = jnp.sum(jnp.where(lane == idxs[kq], before, 0.0), axis=1, keepdims=True)
        meta = jnp.where(lane == kq, idxs[kq].astype(jnp.int32), meta)
        meta = jnp.where(lane == TOP_K + kq, rank.astype(jnp.int32), meta)
        gates = jnp.where(lane == kq, exps[kq] / denom, gates)
    meta_ref[...] = meta
    gate_ref[...] = gates


def _router(h, whi, wlo, bias, tm):
    t, d = h.shape
    return pl.pallas_call(
        _router_kernel,
        out_shape=(jax.ShapeDtypeStruct((t, LANES), jnp.int32),
                   jax.ShapeDtypeStruct((t, LANES), F32),
                   jax.ShapeDtypeStruct((1, LANES), F32)),
        grid=(t // tm,),
        in_specs=[pl.BlockSpec((tm, d), lambda i: (i, 0)),
                  pl.BlockSpec((d, LANES), lambda i: (0, 0)),
                  pl.BlockSpec((d, LANES), lambda i: (0, 0)),
                  pl.BlockSpec((1, LANES), lambda i: (0, 0))],
        out_specs=(pl.BlockSpec((tm, LANES), lambda i: (i, 0)),
                   pl.BlockSpec((tm, LANES), lambda i: (i, 0)),
                   pl.BlockSpec((1, LANES), lambda i: (0, 0))),
        scratch_shapes=[pltpu.VMEM((1, LANES), F32)],
        compiler_params=_cparams(("arbitrary",)),
        name="router",
    )(h, whi, wlo, bias)


def _expert_gather(rows_ref, h_hbm, xbuf, sem, slot, tm):
    for r in range(tm):
        pltpu.make_async_copy(h_hbm.at[rows_ref[0, 0, r]], xbuf.at[slot, r], sem.at[slot]).start()


def _expert_kernel(te_ref, nv_ref, rows_ref, rows_next_ref, h_hbm, wgu_ref, bgu_ref, wd_ref, bd_ref,
                   y_ref, xbuf, sem, *, ff):
    g = pl.program_id(0)
    ng = pl.num_programs(0)
    tm = xbuf.shape[1]
    slot = lax.rem(g, 2)

    @pl.when(g == 0)
    def _():
        _expert_gather(rows_ref, h_hbm, xbuf, sem, 0, tm)

    @pl.when(g + 1 < ng)
    def _():
        _expert_gather(rows_next_ref, h_hbm, xbuf, sem, 1 - slot, tm)

    pltpu.make_async_copy(h_hbm.at[pl.ds(0, tm)], xbuf.at[slot], sem.at[slot]).wait()

    @pl.when(g < nv_ref[0])
    def _():
        x = xbuf[slot].astype(BF16)
        gu = _dot(x, wgu_ref[0]) + bgu_ref[0]
        gt = jnp.minimum(gu[:, :ff], SWIGLU_LIMIT)
        up = jnp.clip(gu[:, ff:], -SWIGLU_LIMIT, SWIGLU_LIMIT)
        act = (up + 1.0) * (gt * _sigmoid(SWIGLU_ALPHA * gt))
        y_ref[...] = _dot(act.astype(BF16), wd_ref[0]) + bd_ref[0]

    @pl.when(g >= nv_ref[0])
    def _():
        y_ref[...] = jnp.zeros_like(y_ref)


def _experts(te, nv, rows3, h, wgu, bgu, wd, bd, tm):
    ntile = rows3.shape[0]
    t, d = h.shape
    e, _, f2 = wgu.shape
    ff = f2 // 2
    kern = functools.partial(_expert_kernel, ff=ff)
    grid_spec = pltpu.PrefetchScalarGridSpec(
        num_scalar_prefetch=2,
        grid=(ntile,),
        in_specs=[pl.BlockSpec((1, 1, tm), lambda g, te, nv: (g, 0, 0), memory_space=pltpu.SMEM),
                  pl.BlockSpec((1, 1, tm), lambda g, te, nv: (jnp.minimum(g + 1, ntile - 1), 0, 0),
                               memory_space=pltpu.SMEM),
                  pl.BlockSpec(memory_space=pl.ANY),
                  pl.BlockSpec((1, d, f2), lambda g, te, nv: (te[g], 0, 0)),
                  pl.BlockSpec((1, 1, f2), lambda g, te, nv: (te[g], 0, 0)),
                  pl.BlockSpec((1, ff, d), lambda g, te, nv: (te[g], 0, 0)),
                  pl.BlockSpec((1, 1, d), lambda g, te, nv: (te[g], 0, 0))],
        out_specs=pl.BlockSpec((tm, d), lambda g, te, nv: (g, 0)),
        scratch_shapes=[pltpu.VMEM((2, tm, d), F32), pltpu.SemaphoreType.DMA((2,))],
    )
    return pl.pallas_call(
        kern,
        out_shape=jax.ShapeDtypeStruct((ntile * tm, d), F32),
        grid_spec=grid_spec,
        compiler_params=_cparams(("arbitrary",)),
        name="experts",
    )(te, nv, rows3, rows3, h, wgu, bgu, wd, bd)


def _combine_gather(pos_ref, y_hbm, ybuf, sem, slot, tc):
    for r in range(tc):
        for kq in range(TOP_K):
            pltpu.make_async_copy(y_hbm.at[pos_ref[0, 0, r * TOP_K + kq]], ybuf.at[slot, kq, r],
                                  sem.at[slot]).start()


def _combine_kernel(pos_ref, pos_next_ref, y_hbm, gate_ref, h_ref, g_ref, b_ref, of_ref, ob_ref,
                    ybuf, sem, *, alpha):
    i = pl.program_id(0)
    n = pl.num_programs(0)
    tc = ybuf.shape[2]
    slot = lax.rem(i, 2)

    @pl.when(i == 0)
    def _():
        _combine_gather(pos_ref, y_hbm, ybuf, sem, 0, tc)

    @pl.when(i + 1 < n)
    def _():
        _combine_gather(pos_next_ref, y_hbm, ybuf, sem, 1 - slot, tc)

    for kq in range(TOP_K):
        pltpu.make_async_copy(y_hbm.at[pl.ds(0, tc)], ybuf.at[slot, kq], sem.at[slot]).wait()

    gates = gate_ref[...]
    f = gates[:, 0:1] * ybuf[slot, 0]
    for kq in range(1, TOP_K):
        f = f + gates[:, kq:kq + 1] * ybuf[slot, kq]
    hn = _layer_norm_rows(alpha * h_ref[...] + f, g_ref[...], b_ref[...])
    of_ref[...] = hn
    ob_ref[...] = hn.astype(ob_ref.dtype)


def _combine_ln(pos3, y, gates, h, g, b, tc, alpha):
    t, d = h.shape
    n = t // tc
    kern = functools.partial(_combine_kernel, alpha=alpha)
    return pl.pallas_call(
        kern,
        out_shape=(jax.ShapeDtypeStruct((t, d), F32), jax.ShapeDtypeStruct((t, d), BF16)),
        grid=(n,),
        in_specs=[pl.BlockSpec((1, 1, tc * TOP_K), lambda i: (i, 0, 0), memory_space=pltpu.SMEM),
                  pl.BlockSpec((1, 1, tc * TOP_K), lambda i: (jnp.minimum(i + 1, n - 1), 0, 0),
                               memory_space=pltpu.SMEM),
                  pl.BlockSpec(memory_space=pl.ANY),
                  pl.BlockSpec((tc, LANES), lambda i: (i, 0)),
                  pl.BlockSpec((tc, d), lambda i: (i, 0)),
                  pl.BlockSpec((1, d), lambda i: (0, 0)),
                  pl.BlockSpec((1, d), lambda i: (0, 0))],
        out_specs=(pl.BlockSpec((tc, d), lambda i: (i, 0)),
                   pl.BlockSpec((tc, d), lambda i: (i, 0))),
        scratch_shapes=[pltpu.VMEM((2, TOP_K, tc, d), F32), pltpu.SemaphoreType.DMA((2,))],
        compiler_params=_cparams(("arbitrary",)),
        name="combine_ln",
    )(pos3, pos3, y, gates, h, g, b)


def _config(x, p, w_in, b_forget, w_gla_gate, g_gla_norm, w_out, w_router, w_down, w_ple_proj):
    d = x.shape[-1]
    fh = b_forget.shape[1]
    rank = w_gla_gate.shape[1]
    gk = w_gla_gate.shape[2]
    dv = g_gla_norm.shape[1]
    mix = w_out.shape[1]
    width = w_in.shape[2]
    fw = (width - fh - 2 * gk - rank) - 2 * mix
    gv = mix - fw
    gh = gv // dv
    cfg = dict(d=d, fh=fh, rank=rank, gk=gk, dv=dv, mix=mix, fw=fw, gv=gv, gh=gh, dk=gk // gh,
               fdh=fw // fh, e=w_router.shape[2], ff=w_down.shape[2], ple=w_ple_proj.shape[1],
               depth=w_in.shape[0])
    assert cfg["fdh"] == LANES and fh + rank <= LANES and cfg["e"] <= LANES
    cfg["gq_blk"] = 3 * fw // LANES
    cfg["gk_blk"] = cfg["gq_blk"] + gk // LANES
    cfg["gv_blk"] = cfg["gk_blk"] + gk // LANES
    cfg["gr_blk"] = cfg["gv_blk"] + gv // LANES
    return cfg


def _prep_in_weights(w_in_l, b_forget_l, cfg):
    fw, fh, gk, gv, rank = cfg["fw"], cfg["fh"], cfg["gk"], cfg["gv"], cfg["rank"]
    o = 0
    parts = {}
    for name, size in (("fq", fw), ("fk", fw), ("fv", fw), ("ff", fh), ("gq", gk), ("gk", gk),
                       ("gv", gv), ("gr", gv), ("glr", rank)):
        parts[name] = w_in_l[:, o:o + size]
        o += size
    w_main = jnp.concatenate([parts[n] for n in ("fq", "fk", "fv", "gq", "gk", "gv", "gr")],
                             axis=1).astype(BF16)
    d = w_in_l.shape[0]
    w_aux = jnp.concatenate([parts["ff"], parts["glr"],
                             jnp.zeros((d, LANES - fh - rank), F32)], axis=1)
    aux_hi, aux_lo = _split_bf16(w_aux)
    bias = jnp.concatenate([b_forget_l, jnp.zeros((LANES - fh,), F32)])[None, :]
    return w_main, aux_hi, aux_lo, bias


def _routing_tables(meta, counts, cfg, t, tm_e, ntile):
    e = cfg["e"]
    topi = meta[:, :TOP_K]
    rank = meta[:, TOP_K:2 * TOP_K]
    cnt = counts[0, :e].astype(jnp.int32)
    padded = ((cnt + tm_e - 1) // tm_e) * tm_e
    ends = jnp.cumsum(padded)
    starts = ends - padded
    pos = starts[topi] + rank
    tok = jnp.broadcast_to(jnp.arange(t, dtype=jnp.int32)[:, None], (t, TOP_K))
    rows = jnp.zeros((ntile * tm_e,), jnp.int32).at[pos.reshape(-1)].set(tok.reshape(-1))
    tile_start = jnp.arange(ntile, dtype=jnp.int32) * tm_e
    te = jnp.minimum(jnp.searchsorted(ends, tile_start, side="right"), e - 1).astype(jnp.int32)
    nv = (ends[-1] // tm_e).astype(jnp.int32)[None]
    last = jnp.maximum(nv[0] - 1, 0)
    te = jnp.where(tile_start // tm_e < nv[0], te, te[last])
    return pos, rows, te, nv


def _tile(n, pref):
    return pref if n % pref == 0 else n


def kernel(x, p, w_in, b_forget, w_gla_gate, b_gla_gate, g_gla_norm, w_out, g_ln1, b_ln1, w_router,
           b_router, w_gate_up, b_gate_up, w_down, b_down, g_ln2, b_ln2, w_ple_gate, w_ple_proj,
           g_ln3, b_ln3):
    cfg = _config(x, p, w_in, b_forget, w_gla_gate, g_gla_norm, w_out, w_router, w_down, w_ple_proj)
    bsz, seq, d = x.shape
    assert bsz == 1
    t = seq
    depth = cfg["depth"]
    alpha = float((2 * depth) ** 0.25)
    e, gh, dk, dv, rank, fh = cfg["e"], cfg["gh"], cfg["dk"], cfg["dv"], cfg["rank"], cfg["fh"]

    tm_in = _tile(t, 1024)
    tn_in = 512
    tq = _tile(t, 512)
    tm_aux = tq
    tm_ln = _tile(t, 512)
    tm_r = _tile(t, 512)
    tm_e = 256
    tc = 128
    ntile = (t * TOP_K) // tm_e + e

    h = x.reshape(t, d)
    hb = h.astype(BF16)
    for l in range(depth):
        w_main, aux_hi, aux_lo, aux_bias = _prep_in_weights(w_in[l], b_forget[l], cfg)
        z3 = _inproj(hb, w_main, tm_in, tn_in)
        aux, cl, off = _aux_proj(hb, aux_hi, aux_lo, aux_bias, tm_aux)
        nq = t // tq
        cl3 = cl[:, :fh].T.reshape(fh, nq, tq)
        off3 = jnp.broadcast_to(off[:, 0, :fh].T[:, :, None], (fh, nq, tq))
        fox = _fox(z3, cl3, off3, fh, tq)

        wg = jnp.zeros((LANES, cfg["gk"]), F32).at[fh:fh + rank].set(w_gla_gate[l])
        wg = wg.reshape(LANES, gh, dk).transpose(1, 0, 2)
        wg_hi, wg_lo = _split_bf16(wg)
        bg = b_gla_gate[l].reshape(gh, 1, dk)
        gla3 = _gla(z3, aux, wg_hi, wg_lo, bg, g_gla_norm[l][None, :], cfg)

        h, hb = _outproj_ln(fox, gla3, w_out[l].astype(BF16), h, g_ln1[l][None, :],
                            b_ln1[l][None, :], tm_ln, alpha)

        wr = jnp.concatenate([w_router[l], jnp.zeros((d, LANES - e), F32)], axis=1)
        wr_hi, wr_lo = _split_bf16(wr)
        rb = jnp.concatenate([b_router[l], jnp.full((LANES - e,), NEG_BIG, F32)])[None, :]
        meta, gates, counts = _router(h, wr_hi, wr_lo, rb, tm_r)
        pos, rows, te, nv = _routing_tables(meta, counts, cfg, t, tm_e, ntile)
        y = _experts(te, nv, rows.reshape(ntile, 1, tm_e), h, w_gate_up[l].astype(BF16),
                     b_gate_up[l][:, None, :], w_down[l].astype(BF16), b_down[l][:, None, :], tm_e)
        h, hb = _combine_ln(pos.reshape(t // tc, 1, tc * TOP_K), y, gates, h, g_ln2[l][None, :],
                            b_ln2[l][None, :], tc, alpha)

        h, hb = _ple_ln(hb, w_ple_gate[l].astype(BF16), p[l].reshape(t, -1).astype(BF16),
                        w_ple_proj[l].astype(BF16), h, g_ln3[l][None, :], b_ln3[l][None, :],
                        tm_ln, _tile(d, 256), alpha)
    return h.reshape(bsz, seq, d)
```

```python
import functools

import jax
import jax.numpy as jnp
import numpy as np
from jax import lax
from jax.experimental import pallas as pl
from jax.experimental.pallas import tpu as pltpu

TOP_K = 4
GLA_TAU = 16.0
SWIGLU_LIMIT = 7.0
SWIGLU_ALPHA = 1.702
LN_EPS = 1e-5
RMS_EPS = 1e-6

LANES = 128
VMEM_LIMIT_BYTES = 56 * 1024 * 1024
NEG_BIG = -1e30
LOG2E = 1.4426950408889634

GLA_CHUNK = 64
GLA_SUB = 16

F32 = jnp.float32
BF16 = jnp.bfloat16


def _cparams(sem):
    return pltpu.CompilerParams(dimension_semantics=sem, vmem_limit_bytes=VMEM_LIMIT_BYTES)


def _split_bf16(x):
    hi = x.astype(BF16)
    lo = (x - hi.astype(F32)).astype(BF16)
    return hi, lo


def _log_sigmoid(v):
    return jnp.minimum(v, 0.0) - jnp.log(1.0 + jnp.exp(-jnp.abs(v)))


def _sigmoid(v):
    return 1.0 / (1.0 + jnp.exp(-v))


def _dot(a, b):
    return jnp.dot(a, b, preferred_element_type=F32)


def _dot_nt(a, b):
    return lax.dot_general(a, b, (((1,), (1,)), ((), ())), preferred_element_type=F32)


def _dot_tn(a, b):
    return lax.dot_general(a, b, (((0,), (0,)), ((), ())), preferred_element_type=F32)


def _layer_norm_rows(x, g, b):
    mu = jnp.mean(x, axis=-1, keepdims=True)
    xc = x - mu
    var = jnp.mean(xc * xc, axis=-1, keepdims=True)
    return xc * lax.rsqrt(var + LN_EPS) * g + b


def _inproj_kernel(x_ref, w_ref, o_ref):
    acc = _dot(x_ref[...], w_ref[...])
    for c in range(o_ref.shape[0]):
        o_ref[c] = acc[:, c * LANES:(c + 1) * LANES].astype(o_ref.dtype)


def _inproj(xb, w, tm, tn):
    t, d = xb.shape
    n = w.shape[1]
    nc = tn // LANES
    return pl.pallas_call(
        _inproj_kernel,
        out_shape=jax.ShapeDtypeStruct((n // LANES, t, LANES), BF16),
        grid=(t // tm, n // tn),
        in_specs=[pl.BlockSpec((tm, d), lambda i, j: (i, 0)),
                  pl.BlockSpec((d, tn), lambda i, j: (0, j))],
        out_specs=pl.BlockSpec((nc, tm, LANES), lambda i, j: (j, i, 0)),
        compiler_params=_cparams(("parallel", "arbitrary")),
        name="inproj",
    )(xb, w)


def _aux_kernel(x_ref, whi_ref, wlo_ref, bias_ref, aux_ref, cl_ref, off_ref, carry_ref):
    i = pl.program_id(0)

    @pl.when(i == 0)
    def _():
        carry_ref[...] = jnp.zeros_like(carry_ref)

    x = x_ref[...]
    acc = _dot(x, whi_ref[...]) + _dot(x, wlo_ref[...])
    aux_ref[...] = acc
    ls = _log_sigmoid(acc + bias_ref[...])
    tm = x.shape[0]
    r = lax.broadcasted_iota(jnp.int32, (tm, tm), 0)
    c = lax.broadcasted_iota(jnp.int32, (tm, tm), 1)
    tri = (c <= r).astype(BF16)
    hi, lo = _split_bf16(ls)
    cl = _dot(tri, hi) + _dot(tri, lo)
    cl_ref[...] = cl
    off_ref[0] = carry_ref[...]
    carry_ref[...] = carry_ref[...] + cl[tm - 1:tm, :]


def _aux_proj(xb, whi, wlo, bias, tm):
    t, d = xb.shape
    nt = t // tm
    return pl.pallas_call(
        _aux_kernel,
        out_shape=(jax.ShapeDtypeStruct((t, LANES), F32),
                   jax.ShapeDtypeStruct((t, LANES), F32),
                   jax.ShapeDtypeStruct((nt, 1, LANES), F32)),
        grid=(nt,),
        in_specs=[pl.BlockSpec((tm, d), lambda i: (i, 0)),
                  pl.BlockSpec((d, LANES), lambda i: (0, 0)),
                  pl.BlockSpec((d, LANES), lambda i: (0, 0)),
                  pl.BlockSpec((1, LANES), lambda i: (0, 0))],
        out_specs=(pl.BlockSpec((tm, LANES), lambda i: (i, 0)),
                   pl.BlockSpec((tm, LANES), lambda i: (i, 0)),
                   pl.BlockSpec((1, 1, LANES), lambda i: (i, 0, 0))),
        scratch_shapes=[pltpu.VMEM((1, LANES), F32)],
        compiler_params=_cparams(("arbitrary",)),
        name="aux_proj",
    )(xb, whi, wlo, bias)


def _fox_kernel(q_ref, k_ref, v_ref, cl_ref, off_ref, o_ref, qs_ref, m_ref, l_ref, acc_ref,
                *, scale, tk, ts, ahead):
    i = pl.program_id(1)
    tq = q_ref.shape[1]
    nsub = tq // ts
    ndiag = tq // tk
    nfull = i * ndiag
    qs_ref[...] = (q_ref[0].astype(F32) * scale).astype(BF16)
    off_i = off_ref[0, pl.ds(nfull, 1), :]

    m_ref[...] = jnp.full_like(m_ref, NEG_BIG)
    l_ref[...] = jnp.zeros_like(l_ref)
    acc_ref[...] = jnp.zeros_like(acc_ref)

    def kcols_of(r, d):
        if d is None:
            return tk
        return max(0, min(tk, (r + 1) * ts - d * tk))

    def scores(r, k, d):
        return _dot_nt(qs_ref[r * ts:(r + 1) * ts, :], k[:kcols_of(r, d)])

    def softmax_pv(r, s, v, bias, d):
        rows = slice(r * ts, (r + 1) * ts)
        kcols = kcols_of(r, d)
        sc = []
        for c in range(kcols // LANES):
            x = s[:, c * LANES:(c + 1) * LANES] + bias[:, c * LANES:(c + 1) * LANES]
            if d is not None and d * tk + (c + 1) * LANES - 1 > r * ts:
                row = r * ts + lax.broadcasted_iota(jnp.int32, (ts, LANES), 0)
                col = d * tk + c * LANES + lax.broadcasted_iota(jnp.int32, (ts, LANES), 1)
                x = jnp.where(col <= row, x, NEG_BIG)
            sc.append(x)
        mx = functools.reduce(jnp.maximum, sc)
        m_old = m_ref[rows, :]
        m_new = jnp.maximum(m_old, jnp.max(mx, axis=1, keepdims=True))
        alpha = jnp.exp2(m_old - m_new)
        p = [jnp.exp2(x - m_new) for x in sc]
        l_ref[rows, :] = alpha * l_ref[rows, :] + functools.reduce(jnp.add, p)
        pb = jnp.concatenate([x.astype(BF16) for x in p], axis=1)
        acc_ref[rows, :] = alpha * acc_ref[rows, :] + _dot(pb, v[:kcols])
        m_ref[rows, :] = m_new

    def step(j, d):
        start = pl.multiple_of(j * tk, tk)
        k = k_ref[0, pl.ds(start, tk), :]
        v = v_ref[0, pl.ds(start, tk), :]
        bias = cl_ref[0, pl.ds(j, 1), :] - (off_ref[0, pl.ds(j, 1), :] - off_i)
        active = [r for r in range(nsub) if kcols_of(r, d) > 0]
        s = {r: scores(r, k, d) for r in active[:ahead]}
        for n, r in enumerate(active):
            if n + ahead < len(active):
                s[active[n + ahead]] = scores(active[n + ahead], k, d)
            softmax_pv(r, s.pop(r), v, bias, d)

    def body(j, carry):
        step(j, None)
        return carry

    lax.fori_loop(0, nfull, body, 0)
    for d in range(ndiag):
        step(nfull + d, d)
    l = jnp.sum(l_ref[...], axis=1, keepdims=True)
    o_ref[...] = (acc_ref[...] / l).astype(o_ref.dtype)


def _fox(z3, cl3, off3, heads, tq, tk, ts):
    _, t, dh = z3.shape
    nk = t // tk
    assert dh == LANES and tq % tk == 0 and tk % ts == 0 and ts % LANES == 0
    kern = functools.partial(_fox_kernel, scale=float(dh) ** -0.5 * LOG2E, tk=tk, ts=ts, ahead=3)
    return pl.pallas_call(
        kern,
        out_shape=jax.ShapeDtypeStruct((t, heads * dh), BF16),
        grid=(heads, t // tq),
        in_specs=[pl.BlockSpec((1, tq, dh), lambda h, i: (h, i, 0)),
                  pl.BlockSpec((1, t, dh), lambda h, i: (heads + h, 0, 0)),
                  pl.BlockSpec((1, t, dh), lambda h, i: (2 * heads + h, 0, 0)),
                  pl.BlockSpec((1, nk, tk), lambda h, i: (h, 0, 0)),
                  pl.BlockSpec((1, nk, tk), lambda h, i: (h, 0, 0))],
        out_specs=pl.BlockSpec((tq, dh), lambda h, i: (i, h)),
        scratch_shapes=[pltpu.VMEM((tq, dh), BF16), pltpu.VMEM((tq, LANES), F32),
                        pltpu.VMEM((tq, LANES), F32), pltpu.VMEM((tq, dh), F32)],
        compiler_params=_cparams(("parallel", "arbitrary")),
        name="fox_attention",
    )(z3, z3, z3, cl3, off3)


def _gla_kernel(q_ref, k_ref, v_ref, r_ref, aux_ref, wg_hi_ref, wg_lo_ref, bg_ref, gn_ref,
                o_ref, st_ref, b_ref, *, heads, dk, dv):
    step = pl.program_id(0)
    ch = GLA_CHUNK
    sb = GLA_SUB
    nsb = ch // sb
    kc = dk // LANES
    vc = dv // LANES

    @pl.when(step == 0)
    def _():
        st_ref[...] = jnp.zeros_like(st_ref)

    a_hi, a_lo = _split_bf16(aux_ref[...])
    r = lax.broadcasted_iota(jnp.int32, (ch, ch), 0)
    c = lax.broadcasted_iota(jnp.int32, (ch, ch), 1)
    tri = (c <= r).astype(BF16)
    for h in range(heads):
        glog = (_dot(a_hi, wg_hi_ref[h]) + _dot(a_lo, wg_hi_ref[h]) + _dot(a_hi, wg_lo_ref[h])
                + bg_ref[h])
        g = _log_sigmoid(glog) * (1.0 / GLA_TAU)
        g_hi, g_lo = _split_bf16(g)
        b_ref[h] = _dot(tri, g_hi) + _dot(tri, g_lo)

    row = lax.broadcasted_iota(jnp.int32, (ch, 1), 0)
    jrow = lax.broadcasted_iota(jnp.int32, (sb, 1), 0)
    lane = lax.broadcasted_iota(jnp.int32, (sb, LANES), 1)
    qscale = float(dk) ** -0.5

    def head_body(h, carry):
        b = b_ref[h]
        q = jnp.concatenate([q_ref[h * kc + u] for u in range(kc)], axis=1).astype(F32) * qscale
        k = jnp.concatenate([k_ref[h * kc + u] for u in range(kc)], axis=1).astype(F32)
        v = jnp.concatenate([v_ref[h * vc + u] for u in range(vc)], axis=1)
        gr = jnp.concatenate([r_ref[h * vc + u] for u in range(vc)], axis=1).astype(F32)
        st = st_ref[h]

        qe = (q * jnp.exp(b)).astype(BF16)
        o = _dot_nt(qe, st.astype(BF16))
        bl = b[ch - 1:ch, :]
        kd = (k * jnp.exp(bl - b)).astype(BF16)
        st_ref[h] = st * jnp.exp(bl) + _dot_tn(v, kd)

        a_rows = [jnp.zeros((sb, ch), F32)]
        for bi in range(1, nsb):
            lo = bi * sb
            ref = b[lo - 1:lo, :]
            qt = (q[lo:lo + sb] * jnp.exp(b[lo:lo + sb] - ref)).astype(BF16)
            kt = (k * jnp.exp(jnp.where(row < lo, ref - b, NEG_BIG))).astype(BF16)
            a_rows.append(_dot_nt(qt, kt))
        a_off = jnp.concatenate(a_rows, axis=0).astype(BF16)
        o = o + _dot(a_off, v)

        o_diag = []
        for bi in range(nsb):
            lo = bi * sb
            bj = b[lo:lo + sb]
            kj = k[lo:lo + sb]
            at = jnp.zeros((sb, LANES), F32)
            for il in range(sb):
                i = lo + il
                e = jnp.where(jrow <= il, b[i:i + 1] - bj, NEG_BIG)
                tt = kj * jnp.exp(e) * q[i:i + 1]
                a = jnp.sum(tt, axis=1, keepdims=True)
                at = jnp.where(lane == il, a, at)
            od = _dot_tn(at.astype(BF16), v[lo:lo + sb])
            o_diag.append(od[0:sb])
        o = o + jnp.concatenate(o_diag, axis=0)

        ms = jnp.mean(o * o, axis=1, keepdims=True)
        on = o * lax.rsqrt(ms + RMS_EPS) * gn_ref[...]
        o_ref[h] = (on * (gr * _sigmoid(gr))).astype(o_ref.dtype)
        return carry

    lax.fori_loop(0, heads, head_body, 0)


def _gla(z3, aux, wg_hi, wg_lo, bg, gn, cfg):
    heads, dk, dv = cfg["gh"], cfg["dk"], cfg["dv"]
    t = z3.shape[1]
    ch = GLA_CHUNK
    kb = heads * dk // LANES
    vb = heads * dv // LANES
    q0, k0, v0, r0 = cfg["gq_blk"], cfg["gk_blk"], cfg["gv_blk"], cfg["gr_blk"]
    assert q0 % kb == 0 and k0 % kb == 0 and v0 % vb == 0 and r0 % vb == 0
    kern = functools.partial(_gla_kernel, heads=heads, dk=dk, dv=dv)
    return pl.pallas_call(
        kern,
        out_shape=jax.ShapeDtypeStruct((heads, t, dv), BF16),
        grid=(t // ch,),
        in_specs=[pl.BlockSpec((kb, ch, LANES), lambda i: (q0 // kb, i, 0)),
                  pl.BlockSpec((kb, ch, LANES), lambda i: (k0 // kb, i, 0)),
                  pl.BlockSpec((vb, ch, LANES), lambda i: (v0 // vb, i, 0)),
                  pl.BlockSpec((vb, ch, LANES), lambda i: (r0 // vb, i, 0)),
                  pl.BlockSpec((ch, LANES), lambda i: (i, 0)),
                  pl.BlockSpec((heads, LANES, dk), lambda i: (0, 0, 0)),
                  pl.BlockSpec((heads, LANES, dk), lambda i: (0, 0, 0)),
                  pl.BlockSpec((heads, 1, dk), lambda i: (0, 0, 0)),
                  pl.BlockSpec((1, dv), lambda i: (0, 0))],
        out_specs=pl.BlockSpec((heads, ch, dv), lambda i: (0, i, 0)),
        scratch_shapes=[pltpu.VMEM((heads, dv, dk), F32), pltpu.VMEM((heads, ch, dk), F32)],
        compiler_params=_cparams(("arbitrary",)),
        name="gla",
    )(z3, z3, z3, z3, aux, wg_hi, wg_lo, bg, gn)


def _ln_finish(g_ref, b_ref, of_ref, ob_ref, rows):
    tm = of_ref.shape[0]

    def body(r, carry):
        sl = pl.ds(pl.multiple_of(r * rows, rows), rows)
        hn = _layer_norm_rows(of_ref[sl, :], g_ref[...], b_ref[...])
        of_ref[sl, :] = hn
        ob_ref[sl, :] = hn.astype(ob_ref.dtype)
        return carry

    lax.fori_loop(0, tm // rows, body, 0)


def _store_col_tile(of_ref, n, val):
    tn = val.shape[1]
    of_ref[:, pl.ds(pl.multiple_of(n * tn, tn), tn)] = val


def _outproj_kernel(x1_ref, x2_ref, w_ref, h_ref, g_ref, b_ref, of_ref, ob_ref, *, alpha):
    n = pl.program_id(1)
    fw = x1_ref.shape[1]
    gh, _, dv = x2_ref.shape
    acc = _dot(x1_ref[...], w_ref[0:fw, :])
    for hd in range(gh):
        acc = acc + _dot(x2_ref[hd], w_ref[fw + hd * dv:fw + (hd + 1) * dv, :])
    _store_col_tile(of_ref, n, alpha * h_ref[...] + acc)

    @pl.when(n == pl.num_programs(1) - 1)
    def _():
        _ln_finish(g_ref, b_ref, of_ref, ob_ref, 64)


def _outproj_ln(fox, gla3, w, h, g, b, tm, tn, alpha):
    t, fw = fox.shape
    gh, _, dv = gla3.shape
    kdim, d = w.shape
    assert kdim == fw + gh * dv
    kern = functools.partial(_outproj_kernel, alpha=alpha)
    return pl.pallas_call(
        kern,
        out_shape=(jax.ShapeDtypeStruct((t, d), F32), jax.ShapeDtypeStruct((t, d), BF16)),
        grid=(t // tm, d // tn),
        in_specs=[pl.BlockSpec((tm, fw), lambda i, n: (i, 0)),
                  pl.BlockSpec((gh, tm, dv), lambda i, n: (0, i, 0)),
                  pl.BlockSpec((kdim, tn), lambda i, n: (0, n)),
                  pl.BlockSpec((tm, tn), lambda i, n: (i, n)),
                  pl.BlockSpec((1, d), lambda i, n: (0, 0)),
                  pl.BlockSpec((1, d), lambda i, n: (0, 0))],
        out_specs=(pl.BlockSpec((tm, d), lambda i, n: (i, 0)),
                   pl.BlockSpec((tm, d), lambda i, n: (i, 0))),
        compiler_params=_cparams(("parallel", "arbitrary")),
        name="outproj_ln",
    )(fox, gla3, w, h, g, b)


def _ple_kernel(x_ref, w_ref, p_ref, wp_ref, h_ref, g_ref, b_ref, of_ref, ob_ref, *, alpha):
    n = pl.program_id(1)
    gate = _sigmoid(_dot(x_ref[...], w_ref[...]))
    emb = _dot(p_ref[...], wp_ref[...])
    _store_col_tile(of_ref, n, alpha * h_ref[...] + gate * emb)

    @pl.when(n == pl.num_programs(1) - 1)
    def _():
        _ln_finish(g_ref, b_ref, of_ref, ob_ref, 64)


def _ple_ln(hb, w, pb, wp, h, g, b, tm, tn, alpha):
    t, d = h.shape
    r = pb.shape[1]
    kern = functools.partial(_ple_kernel, alpha=alpha)
    return pl.pallas_call(
        kern,
        out_shape=(jax.ShapeDtypeStruct((t, d), F32), jax.ShapeDtypeStruct((t, d), BF16)),
        grid=(t // tm, d // tn),
        in_specs=[pl.BlockSpec((tm, d), lambda i, n: (i, 0)),
                  pl.BlockSpec((d, tn), lambda i, n: (0, n)),
                  pl.BlockSpec((tm, r), lambda i, n: (i, 0)),
                  pl.BlockSpec((r, tn), lambda i, n: (0, n)),
                  pl.BlockSpec((tm, tn), lambda i, n: (i, n)),
                  pl.BlockSpec((1, d), lambda i, n: (0, 0)),
                  pl.BlockSpec((1, d), lambda i, n: (0, 0))],
        out_specs=(pl.BlockSpec((tm, d), lambda i, n: (i, 0)),
                   pl.BlockSpec((tm, d), lambda i, n: (i, 0))),
        compiler_params=_cparams(("parallel", "arbitrary")),
        name="ple_ln",
    )(hb, w, pb, wp, h, g, b)


def _router_kernel(x_ref, whi_ref, wlo_ref, bias_ref, meta_ref, gate_ref, cnt_ref, carry_ref):
    i = pl.program_id(0)

    @pl.when(i == 0)
    def _():
        carry_ref[...] = jnp.zeros_like(carry_ref)

    x_hi, x_lo = _split_bf16(x_ref[...])
    logits = (_dot(x_hi, whi_ref[...]) + _dot(x_lo, whi_ref[...]) + _dot(x_hi, wlo_ref[...])
              + bias_ref[...])
    tm = logits.shape[0]
    lane = lax.broadcasted_iota(jnp.int32, (tm, LANES), 1).astype(F32)
    work = logits
    vals, idxs = [], []
    for _ in range(TOP_K):
        m = jnp.max(work, axis=1, keepdims=True)
        idx = jnp.min(jnp.where(work == m, lane, float(LANES)), axis=1, keepdims=True)
        vals.append(m)
        idxs.append(idx)
        work = jnp.where(lane == idx, 2.0 * NEG_BIG, work)
    exps = [jnp.exp(v - vals[0]) for v in vals]
    denom = exps[0] + exps[1] + exps[2] + exps[3]

    onehot = jnp.zeros((tm, LANES), F32)
    for idx in idxs:
        onehot = onehot + (lane == idx).astype(F32)
    r = lax.broadcasted_iota(jnp.int32, (tm, tm), 0)
    c = lax.broadcasted_iota(jnp.int32, (tm, tm), 1)
    strict = (c < r).astype(BF16)
    before = _dot(strict, onehot.astype(BF16)) + carry_ref[...]
    carry_ref[...] = carry_ref[...] + jnp.sum(onehot, axis=0, keepdims=True)
    cnt_ref[...] = carry_ref[...]

    meta = jnp.zeros((tm, LANES), jnp.int32)
    gates = jnp.zeros((tm, LANES), F32)
    for kq in range(TOP_K):
        rank = jnp.sum(jnp.where(lane == idxs[kq], before, 0.0), axis=1, keepdims=True)
        meta = jnp.where(lane == kq, idxs[kq].astype(jnp.int32), meta)
        meta = jnp.where(lane == TOP_K + kq, rank.astype(jnp.int32), meta)
        gates = jnp.where(lane == kq, exps[kq] / denom, gates)
    meta_ref[...] = meta
    gate_ref[...] = gates


def _router(h, whi, wlo, bias, tm):
    t, d = h.shape
    return pl.pallas_call(
        _router_kernel,
        out_shape=(jax.ShapeDtypeStruct((t, LANES), jnp.int32),
                   jax.ShapeDtypeStruct((t, LANES), F32),
                   jax.ShapeDtypeStruct((1, LANES), F32)),
        grid=(t // tm,),
        in_specs=[pl.BlockSpec((tm, d), lambda i: (i, 0)),
                  pl.BlockSpec((d, LANES), lambda i: (0, 0)),
                  pl.BlockSpec((d, LANES), lambda i: (0, 0)),
                  pl.BlockSpec((1, LANES), lambda i: (0, 0))],
        out_specs=(pl.BlockSpec((tm, LANES), lambda i: (i, 0)),
                   pl.BlockSpec((tm, LANES), lambda i: (i, 0)),
                   pl.BlockSpec((1, LANES), lambda i: (0, 0))),
        scratch_shapes=[pltpu.VMEM((1, LANES), F32)],
        compiler_params=_cparams(("arbitrary",)),
        name="router",
    )(h, whi, wlo, bias)


def _expert_gather(rows_ref, h_hbm, xbuf, sem, slot, tm):
    for r in range(tm):
        pltpu.make_async_copy(h_hbm.at[rows_ref[0, 0, r]], xbuf.at[slot, r], sem.at[slot]).start()


def _expert_kernel(te_ref, nv_ref, rows_ref, rows_next_ref, h_hbm, wgu_ref, bgu_ref, wd_ref, bd_ref,
                   y_ref, xbuf, sem, *, ff):
    g = pl.program_id(0)
    ng = pl.num_programs(0)
    tm = xbuf.shape[1]
    slot = lax.rem(g, 2)

    @pl.when(g == 0)
    def _():
        _expert_gather(rows_ref, h_hbm, xbuf, sem, 0, tm)

    @pl.when(g + 1 < ng)
    def _():
        _expert_gather(rows_next_ref, h_hbm, xbuf, sem, 1 - slot, tm)

    pltpu.make_async_copy(h_hbm.at[pl.ds(0, tm)], xbuf.at[slot], sem.at[slot]).wait()

    @pl.when(g < nv_ref[0])
    def _():
        x = xbuf[slot].astype(BF16)
        gu = _dot(x, wgu_ref[0]) + bgu_ref[0]
        gt = jnp.minimum(gu[:, :ff], SWIGLU_LIMIT)
        up = jnp.clip(gu[:, ff:], -SWIGLU_LIMIT, SWIGLU_LIMIT)
        act = (up + 1.0) * (gt * _sigmoid(SWIGLU_ALPHA * gt))
        y_ref[...] = _dot(act.astype(BF16), wd_ref[0]) + bd_ref[0]

    @pl.when(g >= nv_ref[0])
    def _():
        y_ref[...] = jnp.zeros_like(y_ref)


def _experts(te, nv, rows3, h, wgu, bgu, wd, bd, tm):
    ntile = rows3.shape[0]
    t, d = h.shape
    e, _, f2 = wgu.shape
    ff = f2 // 2
    kern = functools.partial(_expert_kernel, ff=ff)
    grid_spec = pltpu.PrefetchScalarGridSpec(
        num_scalar_prefetch=2,
        grid=(ntile,),
        in_specs=[pl.BlockSpec((1, 1, tm), lambda g, te, nv: (g, 0, 0), memory_space=pltpu.SMEM),
                  pl.BlockSpec((1, 1, tm), lambda g, te, nv: (jnp.minimum(g + 1, ntile - 1), 0, 0),
                               memory_space=pltpu.SMEM),
                  pl.BlockSpec(memory_space=pl.ANY),
                  pl.BlockSpec((1, d, f2), lambda g, te, nv: (te[g], 0, 0)),
                  pl.BlockSpec((1, 1, f2), lambda g, te, nv: (te[g], 0, 0)),
                  pl.BlockSpec((1, ff, d), lambda g, te, nv: (te[g], 0, 0)),
                  pl.BlockSpec((1, 1, d), lambda g, te, nv: (te[g], 0, 0))],
        out_specs=pl.BlockSpec((tm, d), lambda g, te, nv: (g, 0)),
        scratch_shapes=[pltpu.VMEM((2, tm, d), F32), pltpu.SemaphoreType.DMA((2,))],
    )
    return pl.pallas_call(
        kern,
        out_shape=jax.ShapeDtypeStruct((ntile * tm, d), F32),
        grid_spec=grid_spec,
        compiler_params=_cparams(("arbitrary",)),
        name="experts",
    )(te, nv, rows3, rows3, h, wgu, bgu, wd, bd)


def _combine_gather(pos_ref, y_hbm, ybuf, sem, slot, tc):
    for r in range(tc):
        for kq in range(TOP_K):
            pltpu.make_async_copy(y_hbm.at[pos_ref[0, 0, r * TOP_K + kq]], ybuf.at[slot, kq, r],
                                  sem.at[slot]).start()


def _combine_kernel(pos_ref, pos_next_ref, y_hbm, gate_ref, h_ref, g_ref, b_ref, of_ref, ob_ref,
                    ybuf, sem, *, alpha):
    i = pl.program_id(0)
    n = pl.num_programs(0)
    tc = ybuf.shape[2]
    slot = lax.rem(i, 2)

    @pl.when(i == 0)
    def _():
        _combine_gather(pos_ref, y_hbm, ybuf, sem, 0, tc)

    @pl.when(i + 1 < n)
    def _():
        _combine_gather(pos_next_ref, y_hbm, ybuf, sem, 1 - slot, tc)

    for kq in range(TOP_K):
        pltpu.make_async_copy(y_hbm.at[pl.ds(0, tc)], ybuf.at[slot, kq], sem.at[slot]).wait()

    gates = gate_ref[...]
    f = gates[:, 0:1] * ybuf[slot, 0]
    for kq in range(1, TOP_K):
        f = f + gates[:, kq:kq + 1] * ybuf[slot, kq]
    hn = _layer_norm_rows(alpha * h_ref[...] + f, g_ref[...], b_ref[...])
    of_ref[...] = hn
    ob_ref[...] = hn.astype(ob_ref.dtype)


def _combine_ln(pos3, y, gates, h, g, b, tc, alpha):
    t, d = h.shape
    n = t // tc
    kern = functools.partial(_combine_kernel, alpha=alpha)
    return pl.pallas_call(
        kern,
        out_shape=(jax.ShapeDtypeStruct((t, d), F32), jax.ShapeDtypeStruct((t, d), BF16)),
        grid=(n,),
        in_specs=[pl.BlockSpec((1, 1, tc * TOP_K), lambda i: (i, 0, 0), memory_space=pltpu.SMEM),
                  pl.BlockSpec((1, 1, tc * TOP_K), lambda i: (jnp.minimum(i + 1, n - 1), 0, 0),
                               memory_space=pltpu.SMEM),
                  pl.BlockSpec(memory_space=pl.ANY),
                  pl.BlockSpec((tc, LANES), lambda i: (i, 0)),
                  pl.BlockSpec((tc, d), lambda i: (i, 0)),
                  pl.BlockSpec((1, d), lambda i: (0, 0)),
                  pl.BlockSpec((1, d), lambda i: (0, 0))],
        out_specs=(pl.BlockSpec((tc, d), lambda i: (i, 0)),
                   pl.BlockSpec((tc, d), lambda i: (i, 0))),
        scratch_shapes=[pltpu.VMEM((2, TOP_K, tc, d), F32), pltpu.SemaphoreType.DMA((2,))],
        compiler_params=_cparams(("arbitrary",)),
        name="combine_ln",
    )(pos3, pos3, y, gates, h, g, b)


def _config(x, p, w_in, b_forget, w_gla_gate, g_gla_norm, w_out, w_router, w_down, w_ple_proj):
    d = x.shape[-1]
    fh = b_forget.shape[1]
    rank = w_gla_gate.shape[1]
    gk = w_gla_gate.shape[2]
    dv = g_gla_norm.shape[1]
    mix = w_out.shape[1]
    width = w_in.shape[2]
    fw = (width - fh - 2 * gk - rank) - 2 * mix
    gv = mix - fw
    gh = gv // dv
    cfg = dict(d=d, fh=fh, rank=rank, gk=gk, dv=dv, mix=mix, fw=fw, gv=gv, gh=gh, dk=gk // gh,
               fdh=fw // fh, e=w_router.shape[2], ff=w_down.shape[2], ple=w_ple_proj.shape[1],
               depth=w_in.shape[0])
    assert cfg["fdh"] == LANES and fh + rank <= LANES and cfg["e"] <= LANES
    cfg["gq_blk"] = 3 * fw // LANES
    cfg["gk_blk"] = cfg["gq_blk"] + gk // LANES
    cfg["gv_blk"] = cfg["gk_blk"] + gk // LANES
    cfg["gr_blk"] = cfg["gv_blk"] + gv // LANES
    return cfg


def _prep_in_weights(w_in_l, b_forget_l, cfg):
    fw, fh, gk, gv, rank = cfg["fw"], cfg["fh"], cfg["gk"], cfg["gv"], cfg["rank"]
    o = 0
    parts = {}
    for name, size in (("fq", fw), ("fk", fw), ("fv", fw), ("ff", fh), ("gq", gk), ("gk", gk),
                       ("gv", gv), ("gr", gv), ("glr", rank)):
        parts[name] = w_in_l[:, o:o + size]
        o += size
    w_main = jnp.concatenate([parts[n] for n in ("fq", "fk", "fv", "gq", "gk", "gv", "gr")],
                             axis=1).astype(BF16)
    d = w_in_l.shape[0]
    w_aux = jnp.concatenate([parts["ff"], parts["glr"],
                             jnp.zeros((d, LANES - fh - rank), F32)], axis=1)
    aux_hi, aux_lo = _split_bf16(w_aux)
    bias = jnp.concatenate([b_forget_l, jnp.zeros((LANES - fh,), F32)])[None, :]
    return w_main, aux_hi, aux_lo, bias


def _routing_tables(meta, counts, cfg, t, tm_e, ntile):
    e = cfg["e"]
    topi = meta[:, :TOP_K]
    rank = meta[:, TOP_K:2 * TOP_K]
    cnt = counts[0, :e].astype(jnp.int32)
    padded = ((cnt + tm_e - 1) // tm_e) * tm_e
    ends = jnp.cumsum(padded)
    starts = ends - padded
    pos = starts[topi] + rank
    tok = jnp.broadcast_to(jnp.arange(t, dtype=jnp.int32)[:, None], (t, TOP_K))
    rows = jnp.zeros((ntile * tm_e,), jnp.int32).at[pos.reshape(-1)].set(tok.reshape(-1))
    tile_start = jnp.arange(ntile, dtype=jnp.int32) * tm_e
    te = jnp.sum((ends[None, :] <= tile_start[:, None]).astype(jnp.int32), axis=1)
    te = jnp.minimum(te, e - 1)
    nv = (ends[-1] // tm_e).astype(jnp.int32)[None]
    last = jnp.maximum(nv[0] - 1, 0)
    te = jnp.where(tile_start // tm_e < nv[0], te, te[last])
    return pos, rows, te, nv


def _tile(n, pref):
    return pref if n % pref == 0 else n


def kernel(x, p, w_in, b_forget, w_gla_gate, b_gla_gate, g_gla_norm, w_out, g_ln1, b_ln1, w_router,
           b_router, w_gate_up, b_gate_up, w_down, b_down, g_ln2, b_ln2, w_ple_gate, w_ple_proj,
           g_ln3, b_ln3):
    cfg = _config(x, p, w_in, b_forget, w_gla_gate, g_gla_norm, w_out, w_router, w_down, w_ple_proj)
    bsz, seq, d = x.shape
    assert bsz == 1
    t = seq
    depth = cfg["depth"]
    alpha = float((2 * depth) ** 0.25)
    e, gh, dk, dv, rank, fh = cfg["e"], cfg["gh"], cfg["dk"], cfg["dv"], cfg["rank"], cfg["fh"]

    tm_in = _tile(t, 1024)
    tn_in = 512
    tq = _tile(t, 1024)
    tk_fox = _tile(tq, 512)
    tm_aux = tk_fox
    ts_fox = 128
    tm_ln = _tile(t, 512)
    tn_ln = _tile(d, 512)
    tm_r = _tile(t, 512)
    tm_e = 256
    tc = 128
    ntile = (t * TOP_K) // tm_e + e

    h = x.reshape(t, d)
    hb = h.astype(BF16)
    for l in range(depth):
        w_main, aux_hi, aux_lo, aux_bias = _prep_in_weights(w_in[l], b_forget[l], cfg)
        z3 = _inproj(hb, w_main, tm_in, tn_in)
        aux, cl, off = _aux_proj(hb, aux_hi, aux_lo, aux_bias, tm_aux)
        nkv = t // tk_fox
        cl3 = (-LOG2E * cl[:, :fh]).T.reshape(fh, nkv, tk_fox)
        off3 = jnp.broadcast_to((LOG2E * off[:, 0, :fh]).T[:, :, None], (fh, nkv, tk_fox))
        fox = _fox(z3, cl3, off3, fh, tq, tk_fox, ts_fox)

        wg = jnp.zeros((LANES, cfg["gk"]), F32).at[fh:fh + rank].set(w_gla_gate[l])
        wg = wg.reshape(LANES, gh, dk).transpose(1, 0, 2)
        wg_hi, wg_lo = _split_bf16(wg)
        bg = b_gla_gate[l].reshape(gh, 1, dk)
        gla3 = _gla(z3, aux, wg_hi, wg_lo, bg, g_gla_norm[l][None, :], cfg)

        h, hb = _outproj_ln(fox, gla3, w_out[l].astype(BF16), h, g_ln1[l][None, :],
                            b_ln1[l][None, :], tm_ln, tn_ln, alpha)

        wr = jnp.concatenate([w_router[l], jnp.zeros((d, LANES - e), F32)], axis=1)
        wr_hi, wr_lo = _split_bf16(wr)
        rb = jnp.concatenate([b_router[l], jnp.full((LANES - e,), NEG_BIG, F32)])[None, :]
        meta, gates, counts = _router(h, wr_hi, wr_lo, rb, tm_r)
        pos, rows, te, nv = _routing_tables(meta, counts, cfg, t, tm_e, ntile)
        y = _experts(te, nv, rows.reshape(ntile, 1, tm_e), h, w_gate_up[l].astype(BF16),
                     b_gate_up[l][:, None, :], w_down[l].astype(BF16), b_down[l][:, None, :], tm_e)
        h, hb = _combine_ln(pos.reshape(t // tc, 1, tc * TOP_K), y, gates, h, g_ln2[l][None, :],
                            b_ln2[l][None, :], tc, alpha)

        h, hb = _ple_ln(hb, w_ple_gate[l].astype(BF16), p[l].reshape(t, -1).astype(BF16),
                        w_ple_proj[l].astype(BF16), h, g_ln3[l][None, :], b_ln3[l][None, :],
                        tm_ln, tn_ln, alpha)
    return h.reshape(bsz, seq, d)
```

```python
import functools

import jax
import jax.numpy as jnp
import numpy as np
from jax import lax
from jax.experimental import pallas as pl
from jax.experimental.pallas import tpu as pltpu

TOP_K = 4
GLA_TAU = 16.0
SWIGLU_LIMIT = 7.0
SWIGLU_ALPHA = 1.702
LN_EPS = 1e-5
RMS_EPS = 1e-6

LANES = 128
VMEM_LIMIT_BYTES = 56 * 1024 * 1024
NEG_BIG = -1e30
LOG2E = 1.4426950408889634

GLA_CHUNK = 64
GLA_SUB = 16

F32 = jnp.float32
BF16 = jnp.bfloat16


def _cparams(sem):
    return pltpu.CompilerParams(dimension_semantics=sem, vmem_limit_bytes=VMEM_LIMIT_BYTES)


def _split_bf16(x):
    hi = x.astype(BF16)
    lo = (x - hi.astype(F32)).astype(BF16)
    return hi, lo


def _log_sigmoid(v):
    return jnp.minimum(v, 0.0) - jnp.log(1.0 + jnp.exp(-jnp.abs(v)))


def _sigmoid(v):
    return 1.0 / (1.0 + jnp.exp(-v))


def _dot(a, b):
    return jnp.dot(a, b, preferred_element_type=F32)


def _dot_nt(a, b):
    return lax.dot_general(a, b, (((1,), (1,)), ((), ())), preferred_element_type=F32)


def _dot_tn(a, b):
    return lax.dot_general(a, b, (((0,), (0,)), ((), ())), preferred_element_type=F32)


def _layer_norm_rows(x, g, b):
    mu = jnp.mean(x, axis=-1, keepdims=True)
    xc = x - mu
    var = jnp.mean(xc * xc, axis=-1, keepdims=True)
    return xc * lax.rsqrt(var + LN_EPS) * g + b


def _inproj_kernel(x_ref, w_ref, o_ref):
    acc = _dot(x_ref[...], w_ref[...])
    for c in range(o_ref.shape[0]):
        o_ref[c] = acc[:, c * LANES:(c + 1) * LANES].astype(o_ref.dtype)


def _inproj(xb, w, tm, tn):
    t, d = xb.shape
    n = w.shape[1]
    nc = tn // LANES
    return pl.pallas_call(
        _inproj_kernel,
        out_shape=jax.ShapeDtypeStruct((n // LANES, t, LANES), BF16),
        grid=(t // tm, n // tn),
        in_specs=[pl.BlockSpec((tm, d), lambda i, j: (i, 0)),
                  pl.BlockSpec((d, tn), lambda i, j: (0, j))],
        out_specs=pl.BlockSpec((nc, tm, LANES), lambda i, j: (j, i, 0)),
        compiler_params=_cparams(("parallel", "arbitrary")),
        name="inproj",
    )(xb, w)


def _aux_kernel(x_ref, whi_ref, wlo_ref, bias_ref, aux_ref, cl_ref, off_ref, carry_ref):
    i = pl.program_id(0)

    @pl.when(i == 0)
    def _():
        carry_ref[...] = jnp.zeros_like(carry_ref)

    x = x_ref[...]
    acc = _dot(x, whi_ref[...]) + _dot(x, wlo_ref[...])
    aux_ref[...] = acc
    ls = _log_sigmoid(acc + bias_ref[...])
    tm = x.shape[0]
    r = lax.broadcasted_iota(jnp.int32, (tm, tm), 0)
    c = lax.broadcasted_iota(jnp.int32, (tm, tm), 1)
    tri = (c <= r).astype(BF16)
    hi, lo = _split_bf16(ls)
    cl = _dot(tri, hi) + _dot(tri, lo)
    cl_ref[...] = cl
    off_ref[0] = carry_ref[...]
    carry_ref[...] = carry_ref[...] + cl[tm - 1:tm, :]


def _aux_proj(xb, whi, wlo, bias, tm):
    t, d = xb.shape
    nt = t // tm
    return pl.pallas_call(
        _aux_kernel,
        out_shape=(jax.ShapeDtypeStruct((t, LANES), F32),
                   jax.ShapeDtypeStruct((t, LANES), F32),
                   jax.ShapeDtypeStruct((nt, 1, LANES), F32)),
        grid=(nt,),
        in_specs=[pl.BlockSpec((tm, d), lambda i: (i, 0)),
                  pl.BlockSpec((d, LANES), lambda i: (0, 0)),
                  pl.BlockSpec((d, LANES), lambda i: (0, 0)),
                  pl.BlockSpec((1, LANES), lambda i: (0, 0))],
        out_specs=(pl.BlockSpec((tm, LANES), lambda i: (i, 0)),
                   pl.BlockSpec((tm, LANES), lambda i: (i, 0)),
                   pl.BlockSpec((1, 1, LANES), lambda i: (i, 0, 0))),
        scratch_shapes=[pltpu.VMEM((1, LANES), F32)],
        compiler_params=_cparams(("arbitrary",)),
        name="aux_proj",
    )(xb, whi, wlo, bias)


def _fox_kernel(q_ref, k_ref, v_ref, cl_ref, off_ref, o_ref, qs_ref, m_ref, l_ref, acc_ref,
                *, scale, tk, ts, ahead):
    i = pl.program_id(1)
    tq = q_ref.shape[1]
    nsub = tq // ts
    ndiag = tq // tk
    nfull = i * ndiag
    qs_ref[...] = (q_ref[0].astype(F32) * scale).astype(BF16)
    off_i = off_ref[0, pl.ds(nfull, 1), :]

    m_ref[...] = jnp.full_like(m_ref, NEG_BIG)
    l_ref[...] = jnp.zeros_like(l_ref)
    acc_ref[...] = jnp.zeros_like(acc_ref)

    def kcols_of(r, d):
        if d is None:
            return tk
        return max(0, min(tk, (r + 1) * ts - d * tk))

    def scores(r, k, d):
        return _dot_nt(qs_ref[r * ts:(r + 1) * ts, :], k[:kcols_of(r, d)])

    def softmax_pv(r, s, v, bias, d):
        rows = slice(r * ts, (r + 1) * ts)
        kcols = kcols_of(r, d)
        sc = []
        for c in range(kcols // LANES):
            x = s[:, c * LANES:(c + 1) * LANES] + bias[:, c * LANES:(c + 1) * LANES]
            if d is not None and d * tk + (c + 1) * LANES - 1 > r * ts:
                row = r * ts + lax.broadcasted_iota(jnp.int32, (ts, LANES), 0)
                col = d * tk + c * LANES + lax.broadcasted_iota(jnp.int32, (ts, LANES), 1)
                x = jnp.where(col <= row, x, NEG_BIG)
            sc.append(x)
        mx = functools.reduce(jnp.maximum, sc)
        m_old = m_ref[rows, :]
        m_new = jnp.maximum(m_old, jnp.max(mx, axis=1, keepdims=True))
        alpha = jnp.exp2(m_old - m_new)
        p = [jnp.exp2(x - m_new) for x in sc]
        l_ref[rows, :] = alpha * l_ref[rows, :] + functools.reduce(jnp.add, p)
        pb = jnp.concatenate([x.astype(BF16) for x in p], axis=1)
        acc_ref[rows, :] = alpha * acc_ref[rows, :] + _dot(pb, v[:kcols])
        m_ref[rows, :] = m_new

    def step(j, d):
        start = pl.multiple_of(j * tk, tk)
        k = k_ref[0, pl.ds(start, tk), :]
        v = v_ref[0, pl.ds(start, tk), :]
        bias = cl_ref[0, pl.ds(j, 1), :] - (off_ref[0, pl.ds(j, 1), :] - off_i)
        active = [r for r in range(nsub) if kcols_of(r, d) > 0]
        s = {r: scores(r, k, d) for r in active[:ahead]}
        for n, r in enumerate(active):
            if n + ahead < len(active):
                s[active[n + ahead]] = scores(active[n + ahead], k, d)
            softmax_pv(r, s.pop(r), v, bias, d)

    def body(j, carry):
        step(j, None)
        return carry

    lax.fori_loop(0, nfull, body, 0)
    for d in range(ndiag):
        step(nfull + d, d)
    l = jnp.sum(l_ref[...], axis=1, keepdims=True)
    o_ref[...] = (acc_ref[...] / l).astype(o_ref.dtype)


def _fox(z3, cl3, off3, heads, tq, tk, ts):
    _, t, dh = z3.shape
    nk = t // tk
    assert dh == LANES and tq % tk == 0 and tk % ts == 0 and ts % LANES == 0
    kern = functools.partial(_fox_kernel, scale=float(dh) ** -0.5 * LOG2E, tk=tk, ts=ts, ahead=3)
    return pl.pallas_call(
        kern,
        out_shape=jax.ShapeDtypeStruct((t, heads * dh), BF16),
        grid=(heads, t // tq),
        in_specs=[pl.BlockSpec((1, tq, dh), lambda h, i: (h, i, 0)),
                  pl.BlockSpec((1, t, dh), lambda h, i: (heads + h, 0, 0)),
                  pl.BlockSpec((1, t, dh), lambda h, i: (2 * heads + h, 0, 0)),
                  pl.BlockSpec((1, nk, tk), lambda h, i: (h, 0, 0)),
                  pl.BlockSpec((1, nk, tk), lambda h, i: (h, 0, 0))],
        out_specs=pl.BlockSpec((tq, dh), lambda h, i: (i, h)),
        scratch_shapes=[pltpu.VMEM((tq, dh), BF16), pltpu.VMEM((tq, LANES), F32),
                        pltpu.VMEM((tq, LANES), F32), pltpu.VMEM((tq, dh), F32)],
        compiler_params=_cparams(("parallel", "arbitrary")),
        name="fox_attention",
    )(z3, z3, z3, cl3, off3)


def _gla_kernel(q_ref, k_ref, v_ref, r_ref, aux_ref, wg_hi_ref, wg_lo_ref, bg_ref, gn_ref,
                o_ref, st_ref, b_ref, *, heads, dk, dv):
    step = pl.program_id(0)
    ch = GLA_CHUNK
    sb = GLA_SUB
    nsb = ch // sb
    kc = dk // LANES
    vc = dv // LANES

    @pl.when(step == 0)
    def _():
        st_ref[...] = jnp.zeros_like(st_ref)

    a_hi, a_lo = _split_bf16(aux_ref[...])
    r = lax.broadcasted_iota(jnp.int32, (ch, ch), 0)
    c = lax.broadcasted_iota(jnp.int32, (ch, ch), 1)
    tri = (c <= r).astype(BF16)
    for h in range(heads):
        glog = (_dot(a_hi, wg_hi_ref[h]) + _dot(a_lo, wg_hi_ref[h]) + _dot(a_hi, wg_lo_ref[h])
                + bg_ref[h])
        g = _log_sigmoid(glog) * (1.0 / GLA_TAU)
        g_hi, g_lo = _split_bf16(g)
        b_ref[h] = _dot(tri, g_hi) + _dot(tri, g_lo)

    row = lax.broadcasted_iota(jnp.int32, (ch, 1), 0)
    jrow = lax.broadcasted_iota(jnp.int32, (sb, 1), 0)
    lane = lax.broadcasted_iota(jnp.int32, (sb, LANES), 1)
    qscale = float(dk) ** -0.5

    def head_body(h, carry):
        b = b_ref[h]
        q = jnp.concatenate([q_ref[h * kc + u] for u in range(kc)], axis=1).astype(F32) * qscale
        k = jnp.concatenate([k_ref[h * kc + u] for u in range(kc)], axis=1).astype(F32)
        v = jnp.concatenate([v_ref[h * vc + u] for u in range(vc)], axis=1)
        gr = jnp.concatenate([r_ref[h * vc + u] for u in range(vc)], axis=1).astype(F32)
        st = st_ref[h]

        qe = (q * jnp.exp(b)).astype(BF16)
        o = _dot_nt(qe, st.astype(BF16))
        bl = b[ch - 1:ch, :]
        kd = (k * jnp.exp(bl - b)).astype(BF16)
        st_ref[h] = st * jnp.exp(bl) + _dot_tn(v, kd)

        a_rows = [jnp.zeros((sb, ch), F32)]
        for bi in range(1, nsb):
            lo = bi * sb
            ref = b[lo - 1:lo, :]
            qt = (q[lo:lo + sb] * jnp.exp(b[lo:lo + sb] - ref)).astype(BF16)
            kt = (k * jnp.exp(jnp.where(row < lo, ref - b, NEG_BIG))).astype(BF16)
            a_rows.append(_dot_nt(qt, kt))
        a_off = jnp.concatenate(a_rows, axis=0).astype(BF16)
        o = o + _dot(a_off, v)

        o_diag = []
        for bi in range(nsb):
            lo = bi * sb
            bj = b[lo:lo + sb]
            kj = k[lo:lo + sb]
            at = jnp.zeros((sb, LANES), F32)
            for il in range(sb):
                i = lo + il
                e = jnp.where(jrow <= il, b[i:i + 1] - bj, NEG_BIG)
                tt = kj * jnp.exp(e) * q[i:i + 1]
                a = jnp.sum(tt, axis=1, keepdims=True)
                at = jnp.where(lane == il, a, at)
            od = _dot_tn(at.astype(BF16), v[lo:lo + sb])
            o_diag.append(od[0:sb])
        o = o + jnp.concatenate(o_diag, axis=0)

        ms = jnp.mean(o * o, axis=1, keepdims=True)
        on = o * lax.rsqrt(ms + RMS_EPS) * gn_ref[...]
        o_ref[h] = (on * (gr * _sigmoid(gr))).astype(o_ref.dtype)
        return carry

    for h in range(heads):
        head_body(h, 0)


def _gla(z3, aux, wg_hi, wg_lo, bg, gn, cfg):
    heads, dk, dv = cfg["gh"], cfg["dk"], cfg["dv"]
    t = z3.shape[1]
    ch = GLA_CHUNK
    kb = heads * dk // LANES
    vb = heads * dv // LANES
    q0, k0, v0, r0 = cfg["gq_blk"], cfg["gk_blk"], cfg["gv_blk"], cfg["gr_blk"]
    assert q0 % kb == 0 and k0 % kb == 0 and v0 % vb == 0 and r0 % vb == 0
    kern = functools.partial(_gla_kernel, heads=heads, dk=dk, dv=dv)
    return pl.pallas_call(
        kern,
        out_shape=jax.ShapeDtypeStruct((heads, t, dv), BF16),
        grid=(t // ch,),
        in_specs=[pl.BlockSpec((kb, ch, LANES), lambda i: (q0 // kb, i, 0)),
                  pl.BlockSpec((kb, ch, LANES), lambda i: (k0 // kb, i, 0)),
                  pl.BlockSpec((vb, ch, LANES), lambda i: (v0 // vb, i, 0)),
                  pl.BlockSpec((vb, ch, LANES), lambda i: (r0 // vb, i, 0)),
                  pl.BlockSpec((ch, LANES), lambda i: (i, 0)),
                  pl.BlockSpec((heads, LANES, dk), lambda i: (0, 0, 0)),
                  pl.BlockSpec((heads, LANES, dk), lambda i: (0, 0, 0)),
                  pl.BlockSpec((heads, 1, dk), lambda i: (0, 0, 0)),
                  pl.BlockSpec((1, dv), lambda i: (0, 0))],
        out_specs=pl.BlockSpec((heads, ch, dv), lambda i: (0, i, 0)),
        scratch_shapes=[pltpu.VMEM((heads, dv, dk), F32), pltpu.VMEM((heads, ch, dk), F32)],
        compiler_params=_cparams(("arbitrary",)),
        name="gla",
    )(z3, z3, z3, z3, aux, wg_hi, wg_lo, bg, gn)


def _ln_finish(g_ref, b_ref, of_ref, ob_ref, rows):
    tm = of_ref.shape[0]

    def body(r, carry):
        sl = pl.ds(pl.multiple_of(r * rows, rows), rows)
        hn = _layer_norm_rows(of_ref[sl, :], g_ref[...], b_ref[...])
        of_ref[sl, :] = hn
        ob_ref[sl, :] = hn.astype(ob_ref.dtype)
        return carry

    lax.fori_loop(0, tm // rows, body, 0)


def _store_col_tile(of_ref, n, val):
    tn = val.shape[1]
    of_ref[:, pl.ds(pl.multiple_of(n * tn, tn), tn)] = val


def _outproj_kernel(x1_ref, x2_ref, w_ref, h_ref, g_ref, b_ref, of_ref, ob_ref, *, alpha):
    n = pl.program_id(1)
    fw = x1_ref.shape[1]
    gh, _, dv = x2_ref.shape
    acc = _dot(x1_ref[...], w_ref[0:fw, :])
    for hd in range(gh):
        acc = acc + _dot(x2_ref[hd], w_ref[fw + hd * dv:fw + (hd + 1) * dv, :])
    _store_col_tile(of_ref, n, alpha * h_ref[...] + acc)

    @pl.when(n == pl.num_programs(1) - 1)
    def _():
        _ln_finish(g_ref, b_ref, of_ref, ob_ref, 64)


def _outproj_ln(fox, gla3, w, h, g, b, tm, tn, alpha):
    t, fw = fox.shape
    gh, _, dv = gla3.shape
    kdim, d = w.shape
    assert kdim == fw + gh * dv
    kern = functools.partial(_outproj_kernel, alpha=alpha)
    return pl.pallas_call(
        kern,
        out_shape=(jax.ShapeDtypeStruct((t, d), F32), jax.ShapeDtypeStruct((t, d), BF16)),
        grid=(t // tm, d // tn),
        in_specs=[pl.BlockSpec((tm, fw), lambda i, n: (i, 0)),
                  pl.BlockSpec((gh, tm, dv), lambda i, n: (0, i, 0)),
                  pl.BlockSpec((kdim, tn), lambda i, n: (0, n)),
                  pl.BlockSpec((tm, tn), lambda i, n: (i, n)),
                  pl.BlockSpec((1, d), lambda i, n: (0, 0)),
                  pl.BlockSpec((1, d), lambda i, n: (0, 0))],
        out_specs=(pl.BlockSpec((tm, d), lambda i, n: (i, 0)),
                   pl.BlockSpec((tm, d), lambda i, n: (i, 0))),
        compiler_params=_cparams(("parallel", "arbitrary")),
        name="outproj_ln",
    )(fox, gla3, w, h, g, b)


def _ple_kernel(x_ref, w_ref, p_ref, wp_ref, h_ref, g_ref, b_ref, of_ref, ob_ref, *, alpha):
    n = pl.program_id(1)
    gate = _sigmoid(_dot(x_ref[...], w_ref[...]))
    emb = _dot(p_ref[...], wp_ref[...])
    _store_col_tile(of_ref, n, alpha * h_ref[...] + gate * emb)

    @pl.when(n == pl.num_programs(1) - 1)
    def _():
        _ln_finish(g_ref, b_ref, of_ref, ob_ref, 64)


def _ple_ln(hb, w, pb, wp, h, g, b, tm, tn, alpha):
    t, d = h.shape
    r = pb.shape[1]
    kern = functools.partial(_ple_kernel, alpha=alpha)
    return pl.pallas_call(
        kern,
        out_shape=(jax.ShapeDtypeStruct((t, d), F32), jax.ShapeDtypeStruct((t, d), BF16)),
        grid=(t // tm, d // tn),
        in_specs=[pl.BlockSpec((tm, d), lambda i, n: (i, 0)),
                  pl.BlockSpec((d, tn), lambda i, n: (0, n)),
                  pl.BlockSpec((tm, r), lambda i, n: (i, 0)),
                  pl.BlockSpec((r, tn), lambda i, n: (0, n)),
                  pl.BlockSpec((tm, tn), lambda i, n: (i, n)),
                  pl.BlockSpec((1, d), lambda i, n: (0, 0)),
                  pl.BlockSpec((1, d), lambda i, n: (0, 0))],
        out_specs=(pl.BlockSpec((tm, d), lambda i, n: (i, 0)),
                   pl.BlockSpec((tm, d), lambda i, n: (i, 0))),
        compiler_params=_cparams(("parallel", "arbitrary")),
        name="ple_ln",
    )(hb, w, pb, wp, h, g, b)


def _router_kernel(x_ref, whi_ref, wlo_ref, bias_ref, meta_ref, gate_ref, cnt_ref, carry_ref):
    i = pl.program_id(0)

    @pl.when(i == 0)
    def _():
        carry_ref[...] = jnp.zeros_like(carry_ref)

    x_hi, x_lo = _split_bf16(x_ref[...])
    logits = (_dot(x_hi, whi_ref[...]) + _dot(x_lo, whi_ref[...]) + _dot(x_hi, wlo_ref[...])
              + bias_ref[...])
    tm = logits.shape[0]
    lane = lax.broadcasted_iota(jnp.int32, (tm, LANES), 1).astype(F32)
    work = logits
    vals, idxs = [], []
    for _ in range(TOP_K):
        m = jnp.max(work, axis=1, keepdims=True)
        idx = jnp.min(jnp.where(work == m, lane, float(LANES)), axis=1, keepdims=True)
        vals.append(m)
        idxs.append(idx)
        work = jnp.where(lane == idx, 2.0 * NEG_BIG, work)
    exps = [jnp.exp(v - vals[0]) for v in vals]
    denom = exps[0] + exps[1] + exps[2] + exps[3]

    onehot = jnp.zeros((tm, LANES), F32)
    for idx in idxs:
        onehot = onehot + (lane == idx).astype(F32)
    r = lax.broadcasted_iota(jnp.int32, (tm, tm), 0)
    c = lax.broadcasted_iota(jnp.int32, (tm, tm), 1)
    strict = (c < r).astype(BF16)
    before = _dot(strict, onehot.astype(BF16)) + carry_ref[...]
    carry_ref[...] = carry_ref[...] + jnp.sum(onehot, axis=0, keepdims=True)
    cnt_ref[...] = carry_ref[...]

    meta = jnp.zeros((tm, LANES), jnp.int32)
    gates = jnp.zeros((tm, LANES), F32)
    for kq in range(TOP_K):
        rank = jnp.sum(jnp.where(lane == idxs[kq], before, 0.0), axis=1, keepdims=True)
        meta = jnp.where(lane == kq, idxs[kq].astype(jnp.int32), meta)
        meta = jnp.where(lane == TOP_K + kq, rank.astype(jnp.int32), meta)
        gates = jnp.where(lane == kq, exps[kq] / denom, gates)
    meta_ref[...] = meta
    gate_ref[...] = gates


def _router(h, whi, wlo, bias, tm):
    t, d = h.shape
    return pl.pallas_call(
        _router_kernel,
        out_shape=(jax.ShapeDtypeStruct((t, LANES), jnp.int32),
                   jax.ShapeDtypeStruct((t, LANES), F32),
                   jax.ShapeDtypeStruct((1, LANES), F32)),
        grid=(t // tm,),
        in_specs=[pl.BlockSpec((tm, d), lambda i: (i, 0)),
                  pl.BlockSpec((d, LANES), lambda i: (0, 0)),
                  pl.BlockSpec((d, LANES), lambda i: (0, 0)),
                  pl.BlockSpec((1, LANES), lambda i: (0, 0))],
        out_specs=(pl.BlockSpec((tm, LANES), lambda i: (i, 0)),
                   pl.BlockSpec((tm, LANES), lambda i: (i, 0)),
                   pl.BlockSpec((1, LANES), lambda i: (0, 0))),
        scratch_shapes=[pltpu.VMEM((1, LANES), F32)],
        compiler_params=_cparams(("arbitrary",)),
        name="router",
    )(h, whi, wlo, bias)


def _expert_gather(rows_ref, h_hbm, xbuf, sem, slot, tm):
    for r in range(tm):
        pltpu.make_async_copy(h_hbm.at[rows_ref[0, 0, r]], xbuf.at[slot, r], sem.at[slot]).start()


def _expert_kernel(te_ref, nv_ref, rows_ref, rows_next_ref, h_hbm, wgu_ref, bgu_ref, wd_ref, bd_ref,
                   y_ref, xbuf, sem, *, ff):
    g = pl.program_id(0)
    ng = pl.num_programs(0)
    tm = xbuf.shape[1]
    slot = lax.rem(g, 2)

    @pl.when(g == 0)
    def _():
        _expert_gather(rows_ref, h_hbm, xbuf, sem, 0, tm)

    @pl.when(g + 1 < ng)
    def _():
        _expert_gather(rows_next_ref, h_hbm, xbuf, sem, 1 - slot, tm)

    pltpu.make_async_copy(h_hbm.at[pl.ds(0, tm)], xbuf.at[slot], sem.at[slot]).wait()

    @pl.when(g < nv_ref[0])
    def _():
        x = xbuf[slot].astype(BF16)
        gu = _dot(x, wgu_ref[0]) + bgu_ref[0]
        gt = jnp.minimum(gu[:, :ff], SWIGLU_LIMIT)
        up = jnp.clip(gu[:, ff:], -SWIGLU_LIMIT, SWIGLU_LIMIT)
        act = (up + 1.0) * (gt * _sigmoid(SWIGLU_ALPHA * gt))
        y_ref[...] = _dot(act.astype(BF16), wd_ref[0]) + bd_ref[0]

    @pl.when(g >= nv_ref[0])
    def _():
        y_ref[...] = jnp.zeros_like(y_ref)


def _experts(te, nv, rows3, h, wgu, bgu, wd, bd, tm):
    ntile = rows3.shape[0]
    t, d = h.shape
    e, _, f2 = wgu.shape
    ff = f2 // 2
    kern = functools.partial(_expert_kernel, ff=ff)
    grid_spec = pltpu.PrefetchScalarGridSpec(
        num_scalar_prefetch=2,
        grid=(ntile,),
        in_specs=[pl.BlockSpec((1, 1, tm), lambda g, te, nv: (g, 0, 0), memory_space=pltpu.SMEM),
                  pl.BlockSpec((1, 1, tm), lambda g, te, nv: (jnp.minimum(g + 1, ntile - 1), 0, 0),
                               memory_space=pltpu.SMEM),
                  pl.BlockSpec(memory_space=pl.ANY),
                  pl.BlockSpec((1, d, f2), lambda g, te, nv: (te[g], 0, 0)),
                  pl.BlockSpec((1, 1, f2), lambda g, te, nv: (te[g], 0, 0)),
                  pl.BlockSpec((1, ff, d), lambda g, te, nv: (te[g], 0, 0)),
                  pl.BlockSpec((1, 1, d), lambda g, te, nv: (te[g], 0, 0))],
        out_specs=pl.BlockSpec((tm, d), lambda g, te, nv: (g, 0)),
        scratch_shapes=[pltpu.VMEM((2, tm, d), F32), pltpu.SemaphoreType.DMA((2,))],
    )
    return pl.pallas_call(
        kern,
        out_shape=jax.ShapeDtypeStruct((ntile * tm, d), F32),
        grid_spec=grid_spec,
        compiler_params=_cparams(("arbitrary",)),
        name="experts",
    )(te, nv, rows3, rows3, h, wgu, bgu, wd, bd)


def _combine_gather(pos_ref, y_hbm, ybuf, sem, slot, tc):
    for r in range(tc):
        for kq in range(TOP_K):
            pltpu.make_async_copy(y_hbm.at[pos_ref[0, 0, r * TOP_K + kq]], ybuf.at[slot, kq, r],
                                  sem.at[slot]).start()


def _combine_kernel(pos_ref, pos_next_ref, y_hbm, gate_ref, h_ref, g_ref, b_ref, of_ref, ob_ref,
                    ybuf, sem, *, alpha):
    i = pl.program_id(0)
    n = pl.num_programs(0)
    tc = ybuf.shape[2]
    slot = lax.rem(i, 2)

    @pl.when(i == 0)
    def _():
        _combine_gather(pos_ref, y_hbm, ybuf, sem, 0, tc)

    @pl.when(i + 1 < n)
    def _():
        _combine_gather(pos_next_ref, y_hbm, ybuf, sem, 1 - slot, tc)

    for kq in range(TOP_K):
        pltpu.make_async_copy(y_hbm.at[pl.ds(0, tc)], ybuf.at[slot, kq], sem.at[slot]).wait()

    gates = gate_ref[...]
    f = gates[:, 0:1] * ybuf[slot, 0]
    for kq in range(1, TOP_K):
        f = f + gates[:, kq:kq + 1] * ybuf[slot, kq]
    hn = _layer_norm_rows(alpha * h_ref[...] + f, g_ref[...], b_ref[...])
    of_ref[...] = hn
    ob_ref[...] = hn.astype(ob_ref.dtype)


def _combine_ln(pos3, y, gates, h, g, b, tc, alpha):
    t, d = h.shape
    n = t // tc
    kern = functools.partial(_combine_kernel, alpha=alpha)
    return pl.pallas_call(
        kern,
        out_shape=(jax.ShapeDtypeStruct((t, d), F32), jax.ShapeDtypeStruct((t, d), BF16)),
        grid=(n,),
        in_specs=[pl.BlockSpec((1, 1, tc * TOP_K), lambda i: (i, 0, 0), memory_space=pltpu.SMEM),
                  pl.BlockSpec((1, 1, tc * TOP_K), lambda i: (jnp.minimum(i + 1, n - 1), 0, 0),
                               memory_space=pltpu.SMEM),
                  pl.BlockSpec(memory_space=pl.ANY),
                  pl.BlockSpec((tc, LANES), lambda i: (i, 0)),
                  pl.BlockSpec((tc, d), lambda i: (i, 0)),
                  pl.BlockSpec((1, d), lambda i: (0, 0)),
                  pl.BlockSpec((1, d), lambda i: (0, 0))],
        out_specs=(pl.BlockSpec((tc, d), lambda i: (i, 0)),
                   pl.BlockSpec((tc, d), lambda i: (i, 0))),
        scratch_shapes=[pltpu.VMEM((2, TOP_K, tc, d), F32), pltpu.SemaphoreType.DMA((2,))],
        compiler_params=_cparams(("arbitrary",)),
        name="combine_ln",
    )(pos3, pos3, y, gates, h, g, b)


def _config(x, p, w_in, b_forget, w_gla_gate, g_gla_norm, w_out, w_router, w_down, w_ple_proj):
    d = x.shape[-1]
    fh = b_forget.shape[1]
    rank = w_gla_gate.shape[1]
    gk = w_gla_gate.shape[2]
    dv = g_gla_norm.shape[1]
    mix = w_out.shape[1]
    width = w_in.shape[2]
    fw = (width - fh - 2 * gk - rank) - 2 * mix
    gv = mix - fw
    gh = gv // dv
    cfg = dict(d=d, fh=fh, rank=rank, gk=gk, dv=dv, mix=mix, fw=fw, gv=gv, gh=gh, dk=gk // gh,
               fdh=fw // fh, e=w_router.shape[2], ff=w_down.shape[2], ple=w_ple_proj.shape[1],
               depth=w_in.shape[0])
    assert cfg["fdh"] == LANES and fh + rank <= LANES and cfg["e"] <= LANES
    cfg["gq_blk"] = 3 * fw // LANES
    cfg["gk_blk"] = cfg["gq_blk"] + gk // LANES
    cfg["gv_blk"] = cfg["gk_blk"] + gk // LANES
    cfg["gr_blk"] = cfg["gv_blk"] + gv // LANES
    return cfg


def _prep_in_weights(w_in_l, b_forget_l, cfg):
    fw, fh, gk, gv, rank = cfg["fw"], cfg["fh"], cfg["gk"], cfg["gv"], cfg["rank"]
    o = 0
    parts = {}
    for name, size in (("fq", fw), ("fk", fw), ("fv", fw), ("ff", fh), ("gq", gk), ("gk", gk),
                       ("gv", gv), ("gr", gv), ("glr", rank)):
        parts[name] = w_in_l[:, o:o + size]
        o += size
    w_main = jnp.concatenate([parts[n] for n in ("fq", "fk", "fv", "gq", "gk", "gv", "gr")],
                             axis=1).astype(BF16)
    d = w_in_l.shape[0]
    w_aux = jnp.concatenate([parts["ff"], parts["glr"],
                             jnp.zeros((d, LANES - fh - rank), F32)], axis=1)
    aux_hi, aux_lo = _split_bf16(w_aux)
    bias = jnp.concatenate([b_forget_l, jnp.zeros((LANES - fh,), F32)])[None, :]
    return w_main, aux_hi, aux_lo, bias


def _routing_tables(meta, counts, cfg, t, tm_e, ntile):
    e = cfg["e"]
    topi = meta[:, :TOP_K]
    rank = meta[:, TOP_K:2 * TOP_K]
    cnt = counts[0, :e].astype(jnp.int32)
    padded = ((cnt + tm_e - 1) // tm_e) * tm_e
    ends = jnp.cumsum(padded)
    starts = ends - padded
    pos = starts[topi] + rank
    tok = jnp.broadcast_to(jnp.arange(t, dtype=jnp.int32)[:, None], (t, TOP_K))
    rows = jnp.zeros((ntile * tm_e,), jnp.int32).at[pos.reshape(-1)].set(tok.reshape(-1))
    tile_start = jnp.arange(ntile, dtype=jnp.int32) * tm_e
    te = jnp.sum((ends[None, :] <= tile_start[:, None]).astype(jnp.int32), axis=1)
    te = jnp.minimum(te, e - 1)
    nv = (ends[-1] // tm_e).astype(jnp.int32)[None]
    last = jnp.maximum(nv[0] - 1, 0)
    te = jnp.where(tile_start // tm_e < nv[0], te, te[last])
    return pos, rows, te, nv


def _tile(n, pref):
    return pref if n % pref == 0 else n


def kernel(x, p, w_in, b_forget, w_gla_gate, b_gla_gate, g_gla_norm, w_out, g_ln1, b_ln1, w_router,
           b_router, w_gate_up, b_gate_up, w_down, b_down, g_ln2, b_ln2, w_ple_gate, w_ple_proj,
           g_ln3, b_ln3):
    cfg = _config(x, p, w_in, b_forget, w_gla_gate, g_gla_norm, w_out, w_router, w_down, w_ple_proj)
    bsz, seq, d = x.shape
    assert bsz == 1
    t = seq
    depth = cfg["depth"]
    alpha = float((2 * depth) ** 0.25)
    e, gh, dk, dv, rank, fh = cfg["e"], cfg["gh"], cfg["dk"], cfg["dv"], cfg["rank"], cfg["fh"]

    tm_in = _tile(t, 1024)
    tn_in = 512
    tq = _tile(t, 2048)
    tk_fox = _tile(tq, 1024)
    tm_aux = tk_fox
    ts_fox = 512
    tm_ln = _tile(t, 512)
    tn_ln = _tile(d, 512)
    tm_r = _tile(t, 512)
    tm_e = 128
    tc = 128
    ntile = -(-(t * TOP_K + e * (tm_e - 1)) // tm_e)

    h = x.reshape(t, d)
    hb = h.astype(BF16)
    for l in range(depth):
        w_main, aux_hi, aux_lo, aux_bias = _prep_in_weights(w_in[l], b_forget[l], cfg)
        z3 = _inproj(hb, w_main, tm_in, tn_in)
        aux, cl, off = _aux_proj(hb, aux_hi, aux_lo, aux_bias, tm_aux)
        nkv = t // tk_fox
        cl3 = (-LOG2E * cl[:, :fh]).T.reshape(fh, nkv, tk_fox)
        off3 = jnp.broadcast_to((LOG2E * off[:, 0, :fh]).T[:, :, None], (fh, nkv, tk_fox))
        fox = _fox(z3, cl3, off3, fh, tq, tk_fox, ts_fox)

        wg = jnp.zeros((LANES, cfg["gk"]), F32).at[fh:fh + rank].set(w_gla_gate[l])
        wg = wg.reshape(LANES, gh, dk).transpose(1, 0, 2)
        wg_hi, wg_lo = _split_bf16(wg)
        bg = b_gla_gate[l].reshape(gh, 1, dk)
        gla3 = _gla(z3, aux, wg_hi, wg_lo, bg, g_gla_norm[l][None, :], cfg)

        h, hb = _outproj_ln(fox, gla3, w_out[l].astype(BF16), h, g_ln1[l][None, :],
                            b_ln1[l][None, :], tm_ln, tn_ln, alpha)

        wr = jnp.concatenate([w_router[l], jnp.zeros((d, LANES - e), F32)], axis=1)
        wr_hi, wr_lo = _split_bf16(wr)
        rb = jnp.concatenate([b_router[l], jnp.full((LANES - e,), NEG_BIG, F32)])[None, :]
        meta, gates, counts = _router(h, wr_hi, wr_lo, rb, tm_r)
        pos, rows, te, nv = _routing_tables(meta, counts, cfg, t, tm_e, ntile)
        y = _experts(te, nv, rows.reshape(ntile, 1, tm_e), h, w_gate_up[l].astype(BF16),
                     b_gate_up[l][:, None, :], w_down[l].astype(BF16), b_down[l][:, None, :], tm_e)
        h, hb = _combine_ln(pos.reshape(t // tc, 1, tc * TOP_K), y, gates, h, g_ln2[l][None, :],
                            b_ln2[l][None, :], tc, alpha)

        h, hb = _ple_ln(hb, w_ple_gate[l].astype(BF16), p[l].reshape(t, -1).astype(BF16),
                        w_ple_proj[l].astype(BF16), h, g_ln3[l][None, :], b_ln3[l][None, :],
                        tm_ln, tn_ln, alpha)
    return h.reshape(bsz, seq, d)
```

```python
import functools

import jax
import jax.numpy as jnp
import numpy as np
from jax import lax
from jax.experimental import pallas as pl
from jax.experimental.pallas import tpu as pltpu

TOP_K = 4
GLA_TAU = 16.0
SWIGLU_LIMIT = 7.0
SWIGLU_ALPHA = 1.702
LN_EPS = 1e-5
RMS_EPS = 1e-6

LANES = 128
VMEM_LIMIT_BYTES = 56 * 1024 * 1024
NEG_BIG = -1e30
LOG2E = 1.4426950408889634

GLA_CHUNK = 64
GLA_SUB = 16

F32 = jnp.float32
BF16 = jnp.bfloat16


def _cparams(sem):
    return pltpu.CompilerParams(dimension_semantics=sem, vmem_limit_bytes=VMEM_LIMIT_BYTES)


def _split_bf16(x):
    hi = x.astype(BF16)
    lo = (x - hi.astype(F32)).astype(BF16)
    return hi, lo


def _log_sigmoid(v):
    return jnp.minimum(v, 0.0) - jnp.log(1.0 + jnp.exp(-jnp.abs(v)))


def _sigmoid(v):
    return 1.0 / (1.0 + jnp.exp(-v))


def _dot(a, b):
    return jnp.dot(a, b, preferred_element_type=F32)


def _dot_nt(a, b):
    return lax.dot_general(a, b, (((1,), (1,)), ((), ())), preferred_element_type=F32)


def _dot_tn(a, b):
    return lax.dot_general(a, b, (((0,), (0,)), ((), ())), preferred_element_type=F32)


def _layer_norm_rows(x, g, b):
    mu = jnp.mean(x, axis=-1, keepdims=True)
    xc = x - mu
    var = jnp.mean(xc * xc, axis=-1, keepdims=True)
    return xc * lax.rsqrt(var + LN_EPS) * g + b


def _inproj_kernel(x_ref, w_ref, o_ref):
    acc = _dot(x_ref[...], w_ref[...])
    for c in range(o_ref.shape[0]):
        o_ref[c] = acc[:, c * LANES:(c + 1) * LANES].astype(o_ref.dtype)


def _inproj(xb, w, tm, tn):
    t, d = xb.shape
    n = w.shape[1]
    nc = tn // LANES
    return pl.pallas_call(
        _inproj_kernel,
        out_shape=jax.ShapeDtypeStruct((n // LANES, t, LANES), BF16),
        grid=(t // tm, n // tn),
        in_specs=[pl.BlockSpec((tm, d), lambda i, j: (i, 0)),
                  pl.BlockSpec((d, tn), lambda i, j: (0, j))],
        out_specs=pl.BlockSpec((nc, tm, LANES), lambda i, j: (j, i, 0)),
        compiler_params=_cparams(("parallel", "arbitrary")),
        name="inproj",
    )(xb, w)


def _aux_kernel(x_ref, whi_ref, wlo_ref, bias_ref, aux_ref, cl_ref, off_ref, carry_ref):
    i = pl.program_id(0)

    @pl.when(i == 0)
    def _():
        carry_ref[...] = jnp.zeros_like(carry_ref)

    x = x_ref[...]
    acc = _dot(x, whi_ref[...]) + _dot(x, wlo_ref[...])
    aux_ref[...] = acc
    ls = _log_sigmoid(acc + bias_ref[...])
    tm = x.shape[0]
    r = lax.broadcasted_iota(jnp.int32, (tm, tm), 0)
    c = lax.broadcasted_iota(jnp.int32, (tm, tm), 1)
    tri = (c <= r).astype(BF16)
    hi, lo = _split_bf16(ls)
    cl = _dot(tri, hi) + _dot(tri, lo)
    cl_ref[...] = cl
    off_ref[0] = carry_ref[...]
    carry_ref[...] = carry_ref[...] + cl[tm - 1:tm, :]


def _aux_proj(xb, whi, wlo, bias, tm):
    t, d = xb.shape
    nt = t // tm
    return pl.pallas_call(
        _aux_kernel,
        out_shape=(jax.ShapeDtypeStruct((t, LANES), F32),
                   jax.ShapeDtypeStruct((t, LANES), F32),
                   jax.ShapeDtypeStruct((nt, 1, LANES), F32)),
        grid=(nt,),
        in_specs=[pl.BlockSpec((tm, d), lambda i: (i, 0)),
                  pl.BlockSpec((d, LANES), lambda i: (0, 0)),
                  pl.BlockSpec((d, LANES), lambda i: (0, 0)),
                  pl.BlockSpec((1, LANES), lambda i: (0, 0))],
        out_specs=(pl.BlockSpec((tm, LANES), lambda i: (i, 0)),
                   pl.BlockSpec((tm, LANES), lambda i: (i, 0)),
                   pl.BlockSpec((1, 1, LANES), lambda i: (i, 0, 0))),
        scratch_shapes=[pltpu.VMEM((1, LANES), F32)],
        compiler_params=_cparams(("arbitrary",)),
        name="aux_proj",
    )(xb, whi, wlo, bias)


def _fox_kernel(q_ref, k_ref, v_ref, cl_ref, off_ref, o_ref, qs_ref, m_ref, l_ref, acc_ref,
                *, scale, tk, ts, ahead):
    i = pl.program_id(1)
    tq = q_ref.shape[1]
    nsub = tq // ts
    ndiag = tq // tk
    nfull = i * ndiag
    qs_ref[...] = (q_ref[0].astype(F32) * scale).astype(BF16)
    off_i = off_ref[0, pl.ds(nfull, 1), :]

    m_ref[...] = jnp.full_like(m_ref, NEG_BIG)
    l_ref[...] = jnp.zeros_like(l_ref)
    acc_ref[...] = jnp.zeros_like(acc_ref)

    def kcols_of(r, d):
        if d is None:
            return tk
        return max(0, min(tk, (r + 1) * ts - d * tk))

    def scores(r, k, d):
        return _dot_nt(qs_ref[r * ts:(r + 1) * ts, :], k[:kcols_of(r, d)])

    def softmax_pv(r, s, v, bias, d):
        rows = slice(r * ts, (r + 1) * ts)
        kcols = kcols_of(r, d)
        sc = []
        for c in range(kcols // LANES):
            x = s[:, c * LANES:(c + 1) * LANES] + bias[:, c * LANES:(c + 1) * LANES]
            if d is not None and d * tk + (c + 1) * LANES - 1 > r * ts:
                row = r * ts + lax.broadcasted_iota(jnp.int32, (ts, LANES), 0)
                col = d * tk + c * LANES + lax.broadcasted_iota(jnp.int32, (ts, LANES), 1)
                x = jnp.where(col <= row, x, NEG_BIG)
            sc.append(x)
        mx = functools.reduce(jnp.maximum, sc)
        m_old = m_ref[rows, :]
        m_new = jnp.maximum(m_old, jnp.max(mx, axis=1, keepdims=True))
        alpha = jnp.exp2(m_old - m_new)
        p = [jnp.exp2(x - m_new) for x in sc]
        l_ref[rows, :] = alpha * l_ref[rows, :] + functools.reduce(jnp.add, p)
        pb = jnp.concatenate([x.astype(BF16) for x in p], axis=1)
        acc_ref[rows, :] = alpha * acc_ref[rows, :] + _dot(pb, v[:kcols])
        m_ref[rows, :] = m_new

    def step(j, d):
        start = pl.multiple_of(j * tk, tk)
        k = k_ref[0, pl.ds(start, tk), :]
        v = v_ref[0, pl.ds(start, tk), :]
        bias = cl_ref[0, pl.ds(j, 1), :] - (off_ref[0, pl.ds(j, 1), :] - off_i)
        active = [r for r in range(nsub) if kcols_of(r, d) > 0]
        s = {r: scores(r, k, d) for r in active[:ahead]}
        for n, r in enumerate(active):
            if n + ahead < len(active):
                s[active[n + ahead]] = scores(active[n + ahead], k, d)
            softmax_pv(r, s.pop(r), v, bias, d)

    def body(j, carry):
        step(j, None)
        return carry

    lax.fori_loop(0, nfull, body, 0)
    for d in range(ndiag):
        step(nfull + d, d)
    l = jnp.sum(l_ref[...], axis=1, keepdims=True)
    o_ref[...] = (acc_ref[...] / l).astype(o_ref.dtype)


def _fox(z3, cl3, off3, heads, tq, tk, ts):
    _, t, dh = z3.shape
    nk = t // tk
    assert dh == LANES and tq % tk == 0 and tk % ts == 0 and ts % LANES == 0
    kern = functools.partial(_fox_kernel, scale=float(dh) ** -0.5 * LOG2E, tk=tk, ts=ts, ahead=3)
    return pl.pallas_call(
        kern,
        out_shape=jax.ShapeDtypeStruct((t, heads * dh), BF16),
        grid=(heads, t // tq),
        in_specs=[pl.BlockSpec((1, tq, dh), lambda h, i: (h, i, 0)),
                  pl.BlockSpec((1, t, dh), lambda h, i: (heads + h, 0, 0)),
                  pl.BlockSpec((1, t, dh), lambda h, i: (2 * heads + h, 0, 0)),
                  pl.BlockSpec((1, nk, tk), lambda h, i: (h, 0, 0)),
                  pl.BlockSpec((1, nk, tk), lambda h, i: (h, 0, 0))],
        out_specs=pl.BlockSpec((tq, dh), lambda h, i: (i, h)),
        scratch_shapes=[pltpu.VMEM((tq, dh), BF16), pltpu.VMEM((tq, LANES), F32),
                        pltpu.VMEM((tq, LANES), F32), pltpu.VMEM((tq, dh), F32)],
        compiler_params=_cparams(("parallel", "arbitrary")),
        name="fox_attention",
    )(z3, z3, z3, cl3, off3)


def _gla_kernel(q_ref, k_ref, v_ref, r_ref, aux_ref, wg_hi_ref, wg_lo_ref, bg_ref, gn_ref,
                o_ref, st_ref, b_ref, *, heads, dk, dv):
    step = pl.program_id(0)
    ch = GLA_CHUNK
    sb = GLA_SUB
    nsb = ch // sb
    kc = dk // LANES
    vc = dv // LANES

    @pl.when(step == 0)
    def _():
        st_ref[...] = jnp.zeros_like(st_ref)

    a_hi, a_lo = _split_bf16(aux_ref[...])
    r = lax.broadcasted_iota(jnp.int32, (ch, ch), 0)
    c = lax.broadcasted_iota(jnp.int32, (ch, ch), 1)
    tri = (c <= r).astype(BF16)
    for h in range(heads):
        glog = (_dot(a_hi, wg_hi_ref[h]) + _dot(a_lo, wg_hi_ref[h]) + _dot(a_hi, wg_lo_ref[h])
                + bg_ref[h])
        g = _log_sigmoid(glog) * (1.0 / GLA_TAU)
        g_hi, g_lo = _split_bf16(g)
        b_ref[h] = _dot(tri, g_hi) + _dot(tri, g_lo)

    row = lax.broadcasted_iota(jnp.int32, (ch, 1), 0)
    jrow = lax.broadcasted_iota(jnp.int32, (sb, 1), 0)
    lane = lax.broadcasted_iota(jnp.int32, (sb, LANES), 1)
    qscale = float(dk) ** -0.5

    def head_body(h, carry):
        b = b_ref[h]
        q = jnp.concatenate([q_ref[h * kc + u] for u in range(kc)], axis=1).astype(F32) * qscale
        k = jnp.concatenate([k_ref[h * kc + u] for u in range(kc)], axis=1).astype(F32)
        v = jnp.concatenate([v_ref[h * vc + u] for u in range(vc)], axis=1)
        gr = jnp.concatenate([r_ref[h * vc + u] for u in range(vc)], axis=1).astype(F32)
        st = st_ref[h]

        qe = (q * jnp.exp(b)).astype(BF16)
        o = _dot_nt(qe, st.astype(BF16))
        bl = b[ch - 1:ch, :]
        kd = (k * jnp.exp(bl - b)).astype(BF16)
        st_ref[h] = st * jnp.exp(bl) + _dot_tn(v, kd)

        a_rows = [jnp.zeros((sb, ch), F32)]
        for bi in range(1, nsb):
            lo = bi * sb
            ref = b[lo - 1:lo, :]
            qt = (q[lo:lo + sb] * jnp.exp(b[lo:lo + sb] - ref)).astype(BF16)
            kt = (k * jnp.exp(jnp.where(row < lo, ref - b, NEG_BIG))).astype(BF16)
            a_rows.append(_dot_nt(qt, kt))
        a_off = jnp.concatenate(a_rows, axis=0).astype(BF16)
        o = o + _dot(a_off, v)

        o_diag = []
        for bi in range(nsb):
            lo = bi * sb
            bj = b[lo:lo + sb]
            kj = k[lo:lo + sb]
            at = jnp.zeros((sb, LANES), F32)
            for il in range(sb):
                i = lo + il
                e = jnp.where(jrow <= il, b[i:i + 1] - bj, NEG_BIG)
                tt = kj * jnp.exp(e) * q[i:i + 1]
                a = jnp.sum(tt, axis=1, keepdims=True)
                at = jnp.where(lane == il, a, at)
            od = _dot_tn(at.astype(BF16), v[lo:lo + sb])
            o_diag.append(od[0:sb])
        o = o + jnp.concatenate(o_diag, axis=0)

        ms = jnp.mean(o * o, axis=1, keepdims=True)
        on = o * lax.rsqrt(ms + RMS_EPS) * gn_ref[...]
        o_ref[h] = (on * (gr * _sigmoid(gr))).astype(o_ref.dtype)
        return carry

    for h in range(heads):
        head_body(h, 0)


def _gla(z3, aux, wg_hi, wg_lo, bg, gn, cfg):
    heads, dk, dv = cfg["gh"], cfg["dk"], cfg["dv"]
    t = z3.shape[1]
    ch = GLA_CHUNK
    kb = heads * dk // LANES
    vb = heads * dv // LANES
    q0, k0, v0, r0 = cfg["gq_blk"], cfg["gk_blk"], cfg["gv_blk"], cfg["gr_blk"]
    assert q0 % kb == 0 and k0 % kb == 0 and v0 % vb == 0 and r0 % vb == 0
    kern = functools.partial(_gla_kernel, heads=heads, dk=dk, dv=dv)
    return pl.pallas_call(
        kern,
        out_shape=jax.ShapeDtypeStruct((heads, t, dv), BF16),
        grid=(t // ch,),
        in_specs=[pl.BlockSpec((kb, ch, LANES), lambda i: (q0 // kb, i, 0)),
                  pl.BlockSpec((kb, ch, LANES), lambda i: (k0 // kb, i, 0)),
                  pl.BlockSpec((vb, ch, LANES), lambda i: (v0 // vb, i, 0)),
                  pl.BlockSpec((vb, ch, LANES), lambda i: (r0 // vb, i, 0)),
                  pl.BlockSpec((ch, LANES), lambda i: (i, 0)),
                  pl.BlockSpec((heads, LANES, dk), lambda i: (0, 0, 0)),
                  pl.BlockSpec((heads, LANES, dk), lambda i: (0, 0, 0)),
                  pl.BlockSpec((heads, 1, dk), lambda i: (0, 0, 0)),
                  pl.BlockSpec((1, dv), lambda i: (0, 0))],
        out_specs=pl.BlockSpec((heads, ch, dv), lambda i: (0, i, 0)),
        scratch_shapes=[pltpu.VMEM((heads, dv, dk), F32), pltpu.VMEM((heads, ch, dk), F32)],
        compiler_params=_cparams(("arbitrary",)),
        name="gla",
    )(z3, z3, z3, z3, aux, wg_hi, wg_lo, bg, gn)


def _ln_finish(g_ref, b_ref, of_ref, ob_ref, rows):
    tm = of_ref.shape[0]

    def body(r, carry):
        sl = pl.ds(pl.multiple_of(r * rows, rows), rows)
        hn = _layer_norm_rows(of_ref[sl, :], g_ref[...], b_ref[...])
        of_ref[sl, :] = hn
        ob_ref[sl, :] = hn.astype(ob_ref.dtype)
        return carry

    lax.fori_loop(0, tm // rows, body, 0)


def _store_col_tile(of_ref, n, val):
    tn = val.shape[1]
    of_ref[:, pl.ds(pl.multiple_of(n * tn, tn), tn)] = val


def _outproj_kernel(x1_ref, x2_ref, w_ref, h_ref, g_ref, b_ref, of_ref, ob_ref, *, alpha):
    n = pl.program_id(1)
    fw = x1_ref.shape[1]
    gh, _, dv = x2_ref.shape
    acc = _dot(x1_ref[...], w_ref[0:fw, :])
    for hd in range(gh):
        acc = acc + _dot(x2_ref[hd], w_ref[fw + hd * dv:fw + (hd + 1) * dv, :])
    _store_col_tile(of_ref, n, alpha * h_ref[...] + acc)

    @pl.when(n == pl.num_programs(1) - 1)
    def _():
        _ln_finish(g_ref, b_ref, of_ref, ob_ref, 64)


def _outproj_ln(fox, gla3, w, h, g, b, tm, tn, alpha):
    t, fw = fox.shape
    gh, _, dv = gla3.shape
    kdim, d = w.shape
    assert kdim == fw + gh * dv
    kern = functools.partial(_outproj_kernel, alpha=alpha)
    return pl.pallas_call(
        kern,
        out_shape=(jax.ShapeDtypeStruct((t, d), F32), jax.ShapeDtypeStruct((t, d), BF16)),
        grid=(t // tm, d // tn),
        in_specs=[pl.BlockSpec((tm, fw), lambda i, n: (i, 0)),
                  pl.BlockSpec((gh, tm, dv), lambda i, n: (0, i, 0)),
                  pl.BlockSpec((kdim, tn), lambda i, n: (0, n)),
                  pl.BlockSpec((tm, tn), lambda i, n: (i, n)),
                  pl.BlockSpec((1, d), lambda i, n: (0, 0)),
                  pl.BlockSpec((1, d), lambda i, n: (0, 0))],
        out_specs=(pl.BlockSpec((tm, d), lambda i, n: (i, 0)),
                   pl.BlockSpec((tm, d), lambda i, n: (i, 0))),
        compiler_params=_cparams(("parallel", "arbitrary")),
        name="outproj_ln",
    )(fox, gla3, w, h, g, b)


def _ple_kernel(x_ref, w_ref, p_ref, wp_ref, h_ref, g_ref, b_ref, of_ref, ob_ref, *, alpha):
    n = pl.program_id(1)
    gate = _sigmoid(_dot(x_ref[...], w_ref[...]))
    emb = _dot(p_ref[...], wp_ref[...])
    _store_col_tile(of_ref, n, alpha * h_ref[...] + gate * emb)

    @pl.when(n == pl.num_programs(1) - 1)
    def _():
        _ln_finish(g_ref, b_ref, of_ref, ob_ref, 64)


def _ple_ln(hb, w, pb, wp, h, g, b, tm, tn, alpha):
    t, d = h.shape
    r = pb.shape[1]
    kern = functools.partial(_ple_kernel, alpha=alpha)
    return pl.pallas_call(
        kern,
        out_shape=(jax.ShapeDtypeStruct((t, d), F32), jax.ShapeDtypeStruct((t, d), BF16)),
        grid=(t // tm, d // tn),
        in_specs=[pl.BlockSpec((tm, d), lambda i, n: (i, 0)),
                  pl.BlockSpec((d, tn), lambda i, n: (0, n)),
                  pl.BlockSpec((tm, r), lambda i, n: (i, 0)),
                  pl.BlockSpec((r, tn), lambda i, n: (0, n)),
                  pl.BlockSpec((tm, tn), lambda i, n: (i, n)),
                  pl.BlockSpec((1, d), lambda i, n: (0, 0)),
                  pl.BlockSpec((1, d), lambda i, n: (0, 0))],
        out_specs=(pl.BlockSpec((tm, d), lambda i, n: (i, 0)),
                   pl.BlockSpec((tm, d), lambda i, n: (i, 0))),
        compiler_params=_cparams(("parallel", "arbitrary")),
        name="ple_ln",
    )(hb, w, pb, wp, h, g, b)


def _router_kernel(x_ref, whi_ref, wlo_ref, bias_ref, meta_ref, gate_ref, cnt_ref, carry_ref):
    i = pl.program_id(0)

    @pl.when(i == 0)
    def _():
        carry_ref[...] = jnp.zeros_like(carry_ref)

    x_hi, x_lo = _split_bf16(x_ref[...])
    logits = (_dot(x_hi, whi_ref[...]) + _dot(x_lo, whi_ref[...]) + _dot(x_hi, wlo_ref[...])
              + bias_ref[...])
    tm = logits.shape[0]
    lane = lax.broadcasted_iota(jnp.int32, (tm, LANES), 1).astype(F32)
    work = logits
    vals, idxs = [], []
    for _ in range(TOP_K):
        m = jnp.max(work, axis=1, keepdims=True)
        idx = jnp.min(jnp.where(work == m, lane, float(LANES)), axis=1, keepdims=True)
        vals.append(m)
        idxs.append(idx)
        work = jnp.where(lane == idx, 2.0 * NEG_BIG, work)
    exps = [jnp.exp(v - vals[0]) for v in vals]
    denom = exps[0] + exps[1] + exps[2] + exps[3]

    onehot = jnp.zeros((tm, LANES), F32)
    for idx in idxs:
        onehot = onehot + (lane == idx).astype(F32)
    r = lax.broadcasted_iota(jnp.int32, (tm, tm), 0)
    c = lax.broadcasted_iota(jnp.int32, (tm, tm), 1)
    strict = (c < r).astype(BF16)
    before = _dot(strict, onehot.astype(BF16)) + carry_ref[...]
    carry_ref[...] = carry_ref[...] + jnp.sum(onehot, axis=0, keepdims=True)
    cnt_ref[...] = carry_ref[...]

    meta = jnp.zeros((tm, LANES), jnp.int32)
    gates = jnp.zeros((tm, LANES), F32)
    for kq in range(TOP_K):
        rank = jnp.sum(jnp.where(lane == idxs[kq], before, 0.0), axis=1, keepdims=True)
        meta = jnp.where(lane == kq, idxs[kq].astype(jnp.int32), meta)
        meta = jnp.where(lane == TOP_K + kq, rank.astype(jnp.int32), meta)
        gates = jnp.where(lane == kq, exps[kq] / denom, gates)
    meta_ref[...] = meta
    gate_ref[...] = gates


def _router(h, whi, wlo, bias, tm):
    t, d = h.shape
    return pl.pallas_call(
        _router_kernel,
        out_shape=(jax.ShapeDtypeStruct((t, LANES), jnp.int32),
                   jax.ShapeDtypeStruct((t, LANES), F32),
                   jax.ShapeDtypeStruct((1, LANES), F32)),
        grid=(t // tm,),
        in_specs=[pl.BlockSpec((tm, d), lambda i: (i, 0)),
                  pl.BlockSpec((d, LANES), lambda i: (0, 0)),
                  pl.BlockSpec((d, LANES), lambda i: (0, 0)),
                  pl.BlockSpec((1, LANES), lambda i: (0, 0))],
        out_specs=(pl.BlockSpec((tm, LANES), lambda i: (i, 0)),
                   pl.BlockSpec((tm, LANES), lambda i: (i, 0)),
                   pl.BlockSpec((1, LANES), lambda i: (0, 0))),
        scratch_shapes=[pltpu.VMEM((1, LANES), F32)],
        compiler_params=_cparams(("arbitrary",)),
        name="router",
    )(h, whi, wlo, bias)


def _expert_gather(rows_ref, h_hbm, xbuf, sem, slot, tm):
    for r in range(tm):
        pltpu.make_async_copy(h_hbm.at[rows_ref[0, 0, r]], xbuf.at[slot, r], sem.at[slot]).start()


def _expert_kernel(te_ref, nv_ref, rows_ref, rows_next_ref, h_hbm, wgu_ref, bgu_ref, wd_ref, bd_ref,
                   y_ref, xbuf, sem, *, ff):
    g = pl.program_id(0)
    ng = pl.num_programs(0)
    tm = xbuf.shape[1]
    slot = lax.rem(g, 2)

    @pl.when(g == 0)
    def _():
        _expert_gather(rows_ref, h_hbm, xbuf, sem, 0, tm)

    @pl.when(g + 1 < ng)
    def _():
        _expert_gather(rows_next_ref, h_hbm, xbuf, sem, 1 - slot, tm)

    pltpu.make_async_copy(h_hbm.at[pl.ds(0, tm)], xbuf.at[slot], sem.at[slot]).wait()

    @pl.when(g < nv_ref[0])
    def _():
        x = xbuf[slot].astype(BF16)
        gu = _dot(x, wgu_ref[0]) + bgu_ref[0]
        gt = jnp.minimum(gu[:, :ff], SWIGLU_LIMIT)
        up = jnp.clip(gu[:, ff:], -SWIGLU_LIMIT, SWIGLU_LIMIT)
        act = (up + 1.0) * (gt * _sigmoid(SWIGLU_ALPHA * gt))
        y_ref[...] = _dot(act.astype(BF16), wd_ref[0]) + bd_ref[0]

    @pl.when(g >= nv_ref[0])
    def _():
        y_ref[...] = jnp.zeros_like(y_ref)


def _experts(te, nv, rows3, h, wgu, bgu, wd, bd, tm):
    ntile = rows3.shape[0]
    t, d = h.shape
    e, _, f2 = wgu.shape
    ff = f2 // 2
    kern = functools.partial(_expert_kernel, ff=ff)
    grid_spec = pltpu.PrefetchScalarGridSpec(
        num_scalar_prefetch=2,
        grid=(ntile,),
        in_specs=[pl.BlockSpec((1, 1, tm), lambda g, te, nv: (g, 0, 0), memory_space=pltpu.SMEM),
                  pl.BlockSpec((1, 1, tm), lambda g, te, nv: (jnp.minimum(g + 1, ntile - 1), 0, 0),
                               memory_space=pltpu.SMEM),
                  pl.BlockSpec(memory_space=pl.ANY),
                  pl.BlockSpec((1, d, f2), lambda g, te, nv: (te[g], 0, 0)),
                  pl.BlockSpec((1, 1, f2), lambda g, te, nv: (te[g], 0, 0)),
                  pl.BlockSpec((1, ff, d), lambda g, te, nv: (te[g], 0, 0)),
                  pl.BlockSpec((1, 1, d), lambda g, te, nv: (te[g], 0, 0))],
        out_specs=pl.BlockSpec((tm, d), lambda g, te, nv: (g, 0)),
        scratch_shapes=[pltpu.VMEM((2, tm, d), F32), pltpu.SemaphoreType.DMA((2,))],
    )
    return pl.pallas_call(
        kern,
        out_shape=jax.ShapeDtypeStruct((ntile * tm, d), F32),
        grid_spec=grid_spec,
        compiler_params=_cparams(("arbitrary",)),
        name="experts",
    )(te, nv, rows3, rows3, h, wgu, bgu, wd, bd)


def _combine_gather(pos_ref, y_hbm, ybuf, sem, slot, tc):
    for r in range(tc):
        for kq in range(TOP_K):
            pltpu.make_async_copy(y_hbm.at[pos_ref[0, 0, r * TOP_K + kq]], ybuf.at[slot, kq, r],
                                  sem.at[slot]).start()


def _combine_kernel(pos_ref, pos_next_ref, y_hbm, gate_ref, h_ref, g_ref, b_ref, of_ref, ob_ref,
                    ybuf, sem, *, alpha):
    i = pl.program_id(0)
    n = pl.num_programs(0)
    tc = ybuf.shape[2]
    slot = lax.rem(i, 2)

    @pl.when(i == 0)
    def _():
        _combine_gather(pos_ref, y_hbm, ybuf, sem, 0, tc)

    @pl.when(i + 1 < n)
    def _():
        _combine_gather(pos_next_ref, y_hbm, ybuf, sem, 1 - slot, tc)

    for kq in range(TOP_K):
        pltpu.make_async_copy(y_hbm.at[pl.ds(0, tc)], ybuf.at[slot, kq], sem.at[slot]).wait()

    gates = gate_ref[...]
    f = gates[:, 0:1] * ybuf[slot, 0]
    for kq in range(1, TOP_K):
        f = f + gates[:, kq:kq + 1] * ybuf[slot, kq]
    hn = _layer_norm_rows(alpha * h_ref[...] + f, g_ref[...], b_ref[...])
    of_ref[...] = hn
    ob_ref[...] = hn.astype(ob_ref.dtype)


def _combine_ln(pos3, y, gates, h, g, b, tc, alpha):
    t, d = h.shape
    n = t // tc
    kern = functools.partial(_combine_kernel, alpha=alpha)
    return pl.pallas_call(
        kern,
        out_shape=(jax.ShapeDtypeStruct((t, d), F32), jax.ShapeDtypeStruct((t, d), BF16)),
        grid=(n,),
        in_specs=[pl.BlockSpec((1, 1, tc * TOP_K), lambda i: (i, 0, 0), memory_space=pltpu.SMEM),
                  pl.BlockSpec((1, 1, tc * TOP_K), lambda i: (jnp.minimum(i + 1, n - 1), 0, 0),
                               memory_space=pltpu.SMEM),
                  pl.BlockSpec(memory_space=pl.ANY),
                  pl.BlockSpec((tc, LANES), lambda i: (i, 0)),
                  pl.BlockSpec((tc, d), lambda i: (i, 0)),
                  pl.BlockSpec((1, d), lambda i: (0, 0)),
                  pl.BlockSpec((1, d), lambda i: (0, 0))],
        out_specs=(pl.BlockSpec((tc, d), lambda i: (i, 0)),
                   pl.BlockSpec((tc, d), lambda i: (i, 0))),
        scratch_shapes=[pltpu.VMEM((2, TOP_K, tc, d), F32), pltpu.SemaphoreType.DMA((2,))],
        compiler_params=_cparams(("arbitrary",)),
        name="combine_ln",
    )(pos3, pos3, y, gates, h, g, b)


def _config(x, p, w_in, b_forget, w_gla_gate, g_gla_norm, w_out, w_router, w_down, w_ple_proj):
    d = x.shape[-1]
    fh = b_forget.shape[1]
    rank = w_gla_gate.shape[1]
    gk = w_gla_gate.shape[2]
    dv = g_gla_norm.shape[1]
    mix = w_out.shape[1]
    width = w_in.shape[2]
    fw = (width - fh - 2 * gk - rank) - 2 * mix
    gv = mix - fw
    gh = gv // dv
    cfg = dict(d=d, fh=fh, rank=rank, gk=gk, dv=dv, mix=mix, fw=fw, gv=gv, gh=gh, dk=gk // gh,
               fdh=fw // fh, e=w_router.shape[2], ff=w_down.shape[2], ple=w_ple_proj.shape[1],
               depth=w_in.shape[0])
    assert cfg["fdh"] == LANES and fh + rank <= LANES and cfg["e"] <= LANES
    cfg["gq_blk"] = 3 * fw // LANES
    cfg["gk_blk"] = cfg["gq_blk"] + gk // LANES
    cfg["gv_blk"] = cfg["gk_blk"] + gk // LANES
    cfg["gr_blk"] = cfg["gv_blk"] + gv // LANES
    return cfg


def _prep_in_weights(w_in_l, b_forget_l, cfg):
    fw, fh, gk, gv, rank = cfg["fw"], cfg["fh"], cfg["gk"], cfg["gv"], cfg["rank"]
    o = 0
    parts = {}
    for name, size in (("fq", fw), ("fk", fw), ("fv", fw), ("ff", fh), ("gq", gk), ("gk", gk),
                       ("gv", gv), ("gr", gv), ("glr", rank)):
        parts[name] = w_in_l[:, o:o + size]
        o += size
    w_main = jnp.concatenate([parts[n] for n in ("fq", "fk", "fv", "gq", "gk", "gv", "gr")],
                             axis=1).astype(BF16)
    d = w_in_l.shape[0]
    w_aux = jnp.concatenate([parts["ff"], parts["glr"],
                             jnp.zeros((d, LANES - fh - rank), F32)], axis=1)
    aux_hi, aux_lo = _split_bf16(w_aux)
    bias = jnp.concatenate([b_forget_l, jnp.zeros((LANES - fh,), F32)])[None, :]
    return w_main, aux_hi, aux_lo, bias


def _routing_tables(meta, counts, cfg, t, tm_e, ntile):
    e = cfg["e"]
    topi = meta[:, :TOP_K]
    rank = meta[:, TOP_K:2 * TOP_K]
    cnt = counts[0, :e].astype(jnp.int32)
    padded = ((cnt + tm_e - 1) // tm_e) * tm_e
    ends = jnp.cumsum(padded)
    starts = ends - padded
    pos = starts[topi] + rank
    tok = jnp.broadcast_to(jnp.arange(t, dtype=jnp.int32)[:, None], (t, TOP_K))
    rows = jnp.zeros((ntile * tm_e,), jnp.int32).at[pos.reshape(-1)].set(tok.reshape(-1))
    tile_start = jnp.arange(ntile, dtype=jnp.int32) * tm_e
    te = jnp.sum((ends[None, :] <= tile_start[:, None]).astype(jnp.int32), axis=1)
    te = jnp.minimum(te, e - 1)
    nv = (ends[-1] // tm_e).astype(jnp.int32)[None]
    last = jnp.maximum(nv[0] - 1, 0)
    te = jnp.where(tile_start // tm_e < nv[0], te, te[last])
    return pos, rows, te, nv


def _tile(n, pref):
    return pref if n % pref == 0 else n


def kernel(x, p, w_in, b_forget, w_gla_gate, b_gla_gate, g_gla_norm, w_out, g_ln1, b_ln1, w_router,
           b_router, w_gate_up, b_gate_up, w_down, b_down, g_ln2, b_ln2, w_ple_gate, w_ple_proj,
           g_ln3, b_ln3):
    cfg = _config(x, p, w_in, b_forget, w_gla_gate, g_gla_norm, w_out, w_router, w_down, w_ple_proj)
    bsz, seq, d = x.shape
    assert bsz == 1
    t = seq
    depth = cfg["depth"]
    alpha = float((2 * depth) ** 0.25)
    e, gh, dk, dv, rank, fh = cfg["e"], cfg["gh"], cfg["dk"], cfg["dv"], cfg["rank"], cfg["fh"]

    tm_in = _tile(t, 1024)
    tn_in = 512
    tq = _tile(t, 4096)
    tk_fox = _tile(tq, 1024)
    tm_aux = tk_fox
    ts_fox = 512
    tm_ln = _tile(t, 512)
    tn_ln = _tile(d, 512)
    tm_r = _tile(t, 512)
    tm_e = 128
    tc = 128
    ntile = -(-(t * TOP_K + e * (tm_e - 1)) // tm_e)

    h = x.reshape(t, d)
    hb = h.astype(BF16)
    for l in range(depth):
        w_main, aux_hi, aux_lo, aux_bias = _prep_in_weights(w_in[l], b_forget[l], cfg)
        z3 = _inproj(hb, w_main, tm_in, tn_in)
        aux, cl, off = _aux_proj(hb, aux_hi, aux_lo, aux_bias, tm_aux)
        nkv = t // tk_fox
        cl3 = (-LOG2E * cl[:, :fh]).T.reshape(fh, nkv, tk_fox)
        off3 = jnp.broadcast_to((LOG2E * off[:, 0, :fh]).T[:, :, None], (fh, nkv, tk_fox))
        fox = _fox(z3, cl3, off3, fh, tq, tk_fox, ts_fox)

        wg = jnp.zeros((LANES, cfg["gk"]), F32).at[fh:fh + rank].set(w_gla_gate[l])
        wg = wg.reshape(LANES, gh, dk).transpose(1, 0, 2)
        wg_hi, wg_lo = _split_bf16(wg)
        bg = b_gla_gate[l].reshape(gh, 1, dk)
        gla3 = _gla(z3, aux, wg_hi, wg_lo, bg, g_gla_norm[l][None, :], cfg)

        h, hb = _outproj_ln(fox, gla3, w_out[l].astype(BF16), h, g_ln1[l][None, :],
                            b_ln1[l][None, :], tm_ln, tn_ln, alpha)

        wr = jnp.concatenate([w_router[l], jnp.zeros((d, LANES - e), F32)], axis=1)
        wr_hi, wr_lo = _split_bf16(wr)
        rb = jnp.concatenate([b_router[l], jnp.full((LANES - e,), NEG_BIG, F32)])[None, :]
        meta, gates, counts = _router(h, wr_hi, wr_lo, rb, tm_r)
        pos, rows, te, nv = _routing_tables(meta, counts, cfg, t, tm_e, ntile)
        y = _experts(te, nv, rows.reshape(ntile, 1, tm_e), h, w_gate_up[l].astype(BF16),
                     b_gate_up[l][:, None, :], w_down[l].astype(BF16), b_down[l][:, None, :], tm_e)
        h, hb = _combine_ln(pos.reshape(t // tc, 1, tc * TOP_K), y, gates, h, g_ln2[l][None, :],
                            b_ln2[l][None, :], tc, alpha)

        h, hb = _ple_ln(hb, w_ple_gate[l].astype(BF16), p[l].reshape(t, -1).astype(BF16),
                        w_ple_proj[l].astype(BF16), h, g_ln3[l][None, :], b_ln3[l][None, :],
                        tm_ln, tn_ln, alpha)
    return h.reshape(bsz, seq, d)
```

```python
import functools

import jax
import jax.numpy as jnp
import numpy as np
from jax import lax
from jax.experimental import pallas as pl
from jax.experimental.pallas import tpu as pltpu

TOP_K = 4
GLA_TAU = 16.0
SWIGLU_LIMIT = 7.0
SWIGLU_ALPHA = 1.702
LN_EPS = 1e-5
RMS_EPS = 1e-6

LANES = 128
VMEM_LIMIT_BYTES = 56 * 1024 * 1024
NEG_BIG = -1e30
LOG2E = 1.4426950408889634

GLA_CHUNK = 64
GLA_SUB = 16

F32 = jnp.float32
BF16 = jnp.bfloat16


def _cparams(sem):
    return pltpu.CompilerParams(dimension_semantics=sem, vmem_limit_bytes=VMEM_LIMIT_BYTES)


def _split_bf16(x):
    hi = x.astype(BF16)
    lo = (x - hi.astype(F32)).astype(BF16)
    return hi, lo


def _log_sigmoid(v):
    return jnp.minimum(v, 0.0) - jnp.log(1.0 + jnp.exp(-jnp.abs(v)))


def _sigmoid(v):
    return 1.0 / (1.0 + jnp.exp(-v))


def _dot(a, b):
    return jnp.dot(a, b, preferred_element_type=F32)


def _dot_nt(a, b):
    return lax.dot_general(a, b, (((1,), (1,)), ((), ())), preferred_element_type=F32)


def _dot_tn(a, b):
    return lax.dot_general(a, b, (((0,), (0,)), ((), ())), preferred_element_type=F32)


def _layer_norm_rows(x, g, b):
    mu = jnp.mean(x, axis=-1, keepdims=True)
    xc = x - mu
    var = jnp.mean(xc * xc, axis=-1, keepdims=True)
    return xc * lax.rsqrt(var + LN_EPS) * g + b


def _inproj_kernel(x_ref, w_ref, o_ref):
    acc = _dot(x_ref[...], w_ref[...])
    for c in range(o_ref.shape[0]):
        o_ref[c] = acc[:, c * LANES:(c + 1) * LANES].astype(o_ref.dtype)


def _inproj(xb, w, tm, tn):
    t, d = xb.shape
    n = w.shape[1]
    nc = tn // LANES
    return pl.pallas_call(
        _inproj_kernel,
        out_shape=jax.ShapeDtypeStruct((n // LANES, t, LANES), BF16),
        grid=(t // tm, n // tn),
        in_specs=[pl.BlockSpec((tm, d), lambda i, j: (i, 0)),
                  pl.BlockSpec((d, tn), lambda i, j: (0, j))],
        out_specs=pl.BlockSpec((nc, tm, LANES), lambda i, j: (j, i, 0)),
        compiler_params=_cparams(("parallel", "arbitrary")),
        name="inproj",
    )(xb, w)


def _aux_kernel(x_ref, whi_ref, wlo_ref, bias_ref, aux_ref, cl_ref, off_ref, carry_ref):
    i = pl.program_id(0)

    @pl.when(i == 0)
    def _():
        carry_ref[...] = jnp.zeros_like(carry_ref)

    x = x_ref[...]
    acc = _dot(x, whi_ref[...]) + _dot(x, wlo_ref[...])
    aux_ref[...] = acc
    ls = _log_sigmoid(acc + bias_ref[...])
    tm = x.shape[0]
    r = lax.broadcasted_iota(jnp.int32, (tm, tm), 0)
    c = lax.broadcasted_iota(jnp.int32, (tm, tm), 1)
    tri = (c <= r).astype(BF16)
    hi, lo = _split_bf16(ls)
    cl = _dot(tri, hi) + _dot(tri, lo)
    cl_ref[...] = cl
    off_ref[0] = carry_ref[...]
    carry_ref[...] = carry_ref[...] + cl[tm - 1:tm, :]


def _aux_proj(xb, whi, wlo, bias, tm):
    t, d = xb.shape
    nt = t // tm
    return pl.pallas_call(
        _aux_kernel,
        out_shape=(jax.ShapeDtypeStruct((t, LANES), F32),
                   jax.ShapeDtypeStruct((t, LANES), F32),
                   jax.ShapeDtypeStruct((nt, 1, LANES), F32)),
        grid=(nt,),
        in_specs=[pl.BlockSpec((tm, d), lambda i: (i, 0)),
                  pl.BlockSpec((d, LANES), lambda i: (0, 0)),
                  pl.BlockSpec((d, LANES), lambda i: (0, 0)),
                  pl.BlockSpec((1, LANES), lambda i: (0, 0))],
        out_specs=(pl.BlockSpec((tm, LANES), lambda i: (i, 0)),
                   pl.BlockSpec((tm, LANES), lambda i: (i, 0)),
                   pl.BlockSpec((1, 1, LANES), lambda i: (i, 0, 0))),
        scratch_shapes=[pltpu.VMEM((1, LANES), F32)],
        compiler_params=_cparams(("arbitrary",)),
        name="aux_proj",
    )(xb, whi, wlo, bias)


def _fox_kernel(q_ref, k_ref, v_ref, cl_ref, off_ref, o_ref, qs_ref, m_ref, l_ref, acc_ref,
                *, scale, tk, ts, ahead):
    i = pl.program_id(1)
    tq = q_ref.shape[1]
    nsub = tq // ts
    ndiag = tq // tk
    nfull = i * ndiag
    qs_ref[...] = (q_ref[0].astype(F32) * scale).astype(BF16)
    off_i = off_ref[0, pl.ds(nfull, 1), :]

    m_ref[...] = jnp.full_like(m_ref, NEG_BIG)
    l_ref[...] = jnp.zeros_like(l_ref)
    acc_ref[...] = jnp.zeros_like(acc_ref)

    def kcols_of(r, d):
        if d is None:
            return tk
        return max(0, min(tk, (r + 1) * ts - d * tk))

    def scores(r, k, d):
        return _dot_nt(qs_ref[r * ts:(r + 1) * ts, :], k[:kcols_of(r, d)])

    def softmax_pv(r, s, v, bias, d):
        rows = slice(r * ts, (r + 1) * ts)
        kcols = kcols_of(r, d)
        sc = []
        for c in range(kcols // LANES):
            x = s[:, c * LANES:(c + 1) * LANES] + bias[:, c * LANES:(c + 1) * LANES]
            if d is not None and d * tk + (c + 1) * LANES - 1 > r * ts:
                row = r * ts + lax.broadcasted_iota(jnp.int32, (ts, LANES), 0)
                col = d * tk + c * LANES + lax.broadcasted_iota(jnp.int32, (ts, LANES), 1)
                x = jnp.where(col <= row, x, NEG_BIG)
            sc.append(x)
        mx = functools.reduce(jnp.maximum, sc)
        m_old = m_ref[rows, :]
        m_new = jnp.maximum(m_old, jnp.max(mx, axis=1, keepdims=True))
        alpha = jnp.exp2(m_old - m_new)
        p = [jnp.exp2(x - m_new) for x in sc]
        l_ref[rows, :] = alpha * l_ref[rows, :] + functools.reduce(jnp.add, p)
        pb = jnp.concatenate([x.astype(BF16) for x in p], axis=1)
        acc_ref[rows, :] = alpha * acc_ref[rows, :] + _dot(pb, v[:kcols])
        m_ref[rows, :] = m_new

    def step(j, d):
        start = pl.multiple_of(j * tk, tk)
        k = k_ref[0, pl.ds(start, tk), :]
        v = v_ref[0, pl.ds(start, tk), :]
        bias = cl_ref[0, pl.ds(j, 1), :] - (off_ref[0, pl.ds(j, 1), :] - off_i)
        active = [r for r in range(nsub) if kcols_of(r, d) > 0]
        s = {r: scores(r, k, d) for r in active[:ahead]}
        for n, r in enumerate(active):
            if n + ahead < len(active):
                s[active[n + ahead]] = scores(active[n + ahead], k, d)
            softmax_pv(r, s.pop(r), v, bias, d)

    def body(j, carry):
        step(j, None)
        return carry

    lax.fori_loop(0, nfull, body, 0)
    for d in range(ndiag):
        step(nfull + d, d)
    l = jnp.sum(l_ref[...], axis=1, keepdims=True)
    o_ref[...] = (acc_ref[...] / l).astype(o_ref.dtype)


def _fox(z3, cl3, off3, heads, tq, tk, ts):
    _, t, dh = z3.shape
    nk = t // tk
    assert dh == LANES and tq % tk == 0 and tk % ts == 0 and ts % LANES == 0
    kern = functools.partial(_fox_kernel, scale=float(dh) ** -0.5 * LOG2E, tk=tk, ts=ts, ahead=3)
    return pl.pallas_call(
        kern,
        out_shape=jax.ShapeDtypeStruct((t, heads * dh), BF16),
        grid=(heads, t // tq),
        in_specs=[pl.BlockSpec((1, tq, dh), lambda h, i: (h, i, 0)),
                  pl.BlockSpec((1, t, dh), lambda h, i: (heads + h, 0, 0)),
                  pl.BlockSpec((1, t, dh), lambda h, i: (2 * heads + h, 0, 0)),
                  pl.BlockSpec((1, nk, tk), lambda h, i: (h, 0, 0)),
                  pl.BlockSpec((1, nk, tk), lambda h, i: (h, 0, 0))],
        out_specs=pl.BlockSpec((tq, dh), lambda h, i: (i, h)),
        scratch_shapes=[pltpu.VMEM((tq, dh), BF16), pltpu.VMEM((tq, LANES), F32),
                        pltpu.VMEM((tq, LANES), F32), pltpu.VMEM((tq, dh), F32)],
        compiler_params=_cparams(("parallel", "arbitrary")),
        name="fox_attention",
    )(z3, z3, z3, cl3, off3)


def _gla_kernel(q_ref, k_ref, v_ref, r_ref, aux_ref, wg_hi_ref, wg_lo_ref, bg_ref, gn_ref,
                o_ref, st_ref, b_ref, *, heads, dk, dv):
    step = pl.program_id(0)
    ch = GLA_CHUNK
    sb = GLA_SUB
    nsb = ch // sb
    kc = dk // LANES
    vc = dv // LANES

    @pl.when(step == 0)
    def _():
        st_ref[...] = jnp.zeros_like(st_ref)

    a_hi, a_lo = _split_bf16(aux_ref[...])
    r = lax.broadcasted_iota(jnp.int32, (ch, ch), 0)
    c = lax.broadcasted_iota(jnp.int32, (ch, ch), 1)
    tri = (c <= r).astype(BF16)
    for h in range(heads):
        glog = (_dot(a_hi, wg_hi_ref[h]) + _dot(a_lo, wg_hi_ref[h]) + _dot(a_hi, wg_lo_ref[h])
                + bg_ref[h])
        g = _log_sigmoid(glog) * (1.0 / GLA_TAU)
        g_hi, g_lo = _split_bf16(g)
        b_ref[h] = _dot(tri, g_hi) + _dot(tri, g_lo)

    row = lax.broadcasted_iota(jnp.int32, (ch, 1), 0)
    jrow = lax.broadcasted_iota(jnp.int32, (sb, 1), 0)
    lane = lax.broadcasted_iota(jnp.int32, (sb, LANES), 1)
    qscale = float(dk) ** -0.5

    def head_body(h, carry):
        b = b_ref[h]
        q = jnp.concatenate([q_ref[h * kc + u] for u in range(kc)], axis=1).astype(F32) * qscale
        k = jnp.concatenate([k_ref[h * kc + u] for u in range(kc)], axis=1).astype(F32)
        v = jnp.concatenate([v_ref[h * vc + u] for u in range(vc)], axis=1)
        gr = jnp.concatenate([r_ref[h * vc + u] for u in range(vc)], axis=1).astype(F32)
        st = st_ref[h]

        qe = (q * jnp.exp(b)).astype(BF16)
        o = _dot_nt(qe, st.astype(BF16))
        bl = b[ch - 1:ch, :]
        kd = (k * jnp.exp(bl - b)).astype(BF16)
        st_ref[h] = st * jnp.exp(bl) + _dot_tn(v, kd)

        a_rows = [jnp.zeros((sb, ch), F32)]
        for bi in range(1, nsb):
            lo = bi * sb
            ref = b[lo - 1:lo, :]
            qt = (q[lo:lo + sb] * jnp.exp(b[lo:lo + sb] - ref)).astype(BF16)
            kt = (k * jnp.exp(jnp.where(row < lo, ref - b, NEG_BIG))).astype(BF16)
            a_rows.append(_dot_nt(qt, kt))
        a_off = jnp.concatenate(a_rows, axis=0).astype(BF16)
        o = o + _dot(a_off, v)

        o_diag = []
        for bi in range(nsb):
            lo = bi * sb
            bj = b[lo:lo + sb]
            kj = k[lo:lo + sb]
            at = jnp.zeros((sb, LANES), F32)
            for il in range(sb):
                i = lo + il
                e = jnp.where(jrow <= il, b[i:i + 1] - bj, NEG_BIG)
                tt = kj * jnp.exp(e) * q[i:i + 1]
                a = jnp.sum(tt, axis=1, keepdims=True)
                at = jnp.where(lane == il, a, at)
            od = _dot_tn(at.astype(BF16), v[lo:lo + sb])
            o_diag.append(od[0:sb])
        o = o + jnp.concatenate(o_diag, axis=0)

        ms = jnp.mean(o * o, axis=1, keepdims=True)
        on = o * lax.rsqrt(ms + RMS_EPS) * gn_ref[...]
        o_ref[h] = (on * (gr * _sigmoid(gr))).astype(o_ref.dtype)
        return carry

    for h in range(heads):
        head_body(h, 0)


def _gla(z3, aux, wg_hi, wg_lo, bg, gn, cfg):
    heads, dk, dv = cfg["gh"], cfg["dk"], cfg["dv"]
    t = z3.shape[1]
    ch = GLA_CHUNK
    kb = heads * dk // LANES
    vb = heads * dv // LANES
    q0, k0, v0, r0 = cfg["gq_blk"], cfg["gk_blk"], cfg["gv_blk"], cfg["gr_blk"]
    assert q0 % kb == 0 and k0 % kb == 0 and v0 % vb == 0 and r0 % vb == 0
    kern = functools.partial(_gla_kernel, heads=heads, dk=dk, dv=dv)
    return pl.pallas_call(
        kern,
        out_shape=jax.ShapeDtypeStruct((heads, t, dv), BF16),
        grid=(t // ch,),
        in_specs=[pl.BlockSpec((kb, ch, LANES), lambda i: (q0 // kb, i, 0)),
                  pl.BlockSpec((kb, ch, LANES), lambda i: (k0 // kb, i, 0)),
                  pl.BlockSpec((vb, ch, LANES), lambda i: (v0 // vb, i, 0)),
                  pl.BlockSpec((vb, ch, LANES), lambda i: (r0 // vb, i, 0)),
                  pl.BlockSpec((ch, LANES), lambda i: (i, 0)),
                  pl.BlockSpec((heads, LANES, dk), lambda i: (0, 0, 0)),
                  pl.BlockSpec((heads, LANES, dk), lambda i: (0, 0, 0)),
                  pl.BlockSpec((heads, 1, dk), lambda i: (0, 0, 0)),
                  pl.BlockSpec((1, dv), lambda i: (0, 0))],
        out_specs=pl.BlockSpec((heads, ch, dv), lambda i: (0, i, 0)),
        scratch_shapes=[pltpu.VMEM((heads, dv, dk), F32), pltpu.VMEM((heads, ch, dk), F32)],
        compiler_params=_cparams(("arbitrary",)),
        name="gla",
    )(z3, z3, z3, z3, aux, wg_hi, wg_lo, bg, gn)


def _ln_finish(g_ref, b_ref, of_ref, ob_ref, rows):
    tm = of_ref.shape[0]

    def body(r, carry):
        sl = pl.ds(pl.multiple_of(r * rows, rows), rows)
        hn = _layer_norm_rows(of_ref[sl, :], g_ref[...], b_ref[...])
        of_ref[sl, :] = hn
        ob_ref[sl, :] = hn.astype(ob_ref.dtype)
        return carry

    lax.fori_loop(0, tm // rows, body, 0)


def _store_col_tile(of_ref, n, val):
    tn = val.shape[1]
    of_ref[:, pl.ds(pl.multiple_of(n * tn, tn), tn)] = val


def _outproj_kernel(x1_ref, x2_ref, w_ref, h_ref, g_ref, b_ref, of_ref, ob_ref, *, alpha):
    n = pl.program_id(1)
    fw = x1_ref.shape[1]
    gh, _, dv = x2_ref.shape
    acc = _dot(x1_ref[...], w_ref[0:fw, :])
    for hd in range(gh):
        acc = acc + _dot(x2_ref[hd], w_ref[fw + hd * dv:fw + (hd + 1) * dv, :])
    _store_col_tile(of_ref, n, alpha * h_ref[...] + acc)

    @pl.when(n == pl.num_programs(1) - 1)
    def _():
        _ln_finish(g_ref, b_ref, of_ref, ob_ref, 64)


def _outproj_ln(fox, gla3, w, h, g, b, tm, tn, alpha):
    t, fw = fox.shape
    gh, _, dv = gla3.shape
    kdim, d = w.shape
    assert kdim == fw + gh * dv
    kern = functools.partial(_outproj_kernel, alpha=alpha)
    return pl.pallas_call(
        kern,
        out_shape=(jax.ShapeDtypeStruct((t, d), F32), jax.ShapeDtypeStruct((t, d), BF16)),
        grid=(t // tm, d // tn),
        in_specs=[pl.BlockSpec((tm, fw), lambda i, n: (i, 0)),
                  pl.BlockSpec((gh, tm, dv), lambda i, n: (0, i, 0)),
                  pl.BlockSpec((kdim, tn), lambda i, n: (0, n)),
                  pl.BlockSpec((tm, tn), lambda i, n: (i, n)),
                  pl.BlockSpec((1, d), lambda i, n: (0, 0)),
                  pl.BlockSpec((1, d), lambda i, n: (0, 0))],
        out_specs=(pl.BlockSpec((tm, d), lambda i, n: (i, 0)),
                   pl.BlockSpec((tm, d), lambda i, n: (i, 0))),
        compiler_params=_cparams(("parallel", "arbitrary")),
        name="outproj_ln",
    )(fox, gla3, w, h, g, b)


def _ple_kernel(x_ref, w_ref, p_ref, wp_ref, h_ref, g_ref, b_ref, of_ref, ob_ref, *, alpha):
    n = pl.program_id(1)
    gate = _sigmoid(_dot(x_ref[...], w_ref[...]))
    emb = _dot(p_ref[...], wp_ref[...])
    _store_col_tile(of_ref, n, alpha * h_ref[...] + gate * emb)

    @pl.when(n == pl.num_programs(1) - 1)
    def _():
        _ln_finish(g_ref, b_ref, of_ref, ob_ref, 64)


def _ple_ln(hb, w, pb, wp, h, g, b, tm, tn, alpha):
    t, d = h.shape
    r = pb.shape[1]
    kern = functools.partial(_ple_kernel, alpha=alpha)
    return pl.pallas_call(
        kern,
        out_shape=(jax.ShapeDtypeStruct((t, d), F32), jax.ShapeDtypeStruct((t, d), BF16)),
        grid=(t // tm, d // tn),
        in_specs=[pl.BlockSpec((tm, d), lambda i, n: (i, 0)),
                  pl.BlockSpec((d, tn), lambda i, n: (0, n)),
                  pl.BlockSpec((tm, r), lambda i, n: (i, 0)),
                  pl.BlockSpec((r, tn), lambda i, n: (0, n)),
                  pl.BlockSpec((tm, tn), lambda i, n: (i, n)),
                  pl.BlockSpec((1, d), lambda i, n: (0, 0)),
                  pl.BlockSpec((1, d), lambda i, n: (0, 0))],
        out_specs=(pl.BlockSpec((tm, d), lambda i, n: (i, 0)),
                   pl.BlockSpec((tm, d), lambda i, n: (i, 0))),
        compiler_params=_cparams(("parallel", "arbitrary")),
        name="ple_ln",
    )(hb, w, pb, wp, h, g, b)


def _router_kernel(x_ref, whi_ref, wlo_ref, bias_ref, meta_ref, gate_ref, cnt_ref, carry_ref):
    i = pl.program_id(0)

    @pl.when(i == 0)
    def _():
        carry_ref[...] = jnp.zeros_like(carry_ref)

    x_hi, x_lo = _split_bf16(x_ref[...])
    logits = (_dot(x_hi, whi_ref[...]) + _dot(x_lo, whi_ref[...]) + _dot(x_hi, wlo_ref[...])
              + bias_ref[...])
    tm = logits.shape[0]
    lane = lax.broadcasted_iota(jnp.int32, (tm, LANES), 1).astype(F32)
    work = logits
    vals, idxs = [], []
    for _ in range(TOP_K):
        m = jnp.max(work, axis=1, keepdims=True)
        idx = jnp.min(jnp.where(work == m, lane, float(LANES)), axis=1, keepdims=True)
        vals.append(m)
        idxs.append(idx)
        work = jnp.where(lane == idx, 2.0 * NEG_BIG, work)
    exps = [jnp.exp(v - vals[0]) for v in vals]
    denom = exps[0] + exps[1] + exps[2] + exps[3]

    onehot = jnp.zeros((tm, LANES), F32)
    for idx in idxs:
        onehot = onehot + (lane == idx).astype(F32)
    r = lax.broadcasted_iota(jnp.int32, (tm, tm), 0)
    c = lax.broadcasted_iota(jnp.int32, (tm, tm), 1)
    strict = (c < r).astype(BF16)
    before = _dot(strict, onehot.astype(BF16)) + carry_ref[...]
    carry_ref[...] = carry_ref[...] + jnp.sum(onehot, axis=0, keepdims=True)
    cnt_ref[...] = carry_ref[...]

    meta = jnp.zeros((tm, LANES), jnp.int32)
    gates = jnp.zeros((tm, LANES), F32)
    for kq in range(TOP_K):
        rank = jnp.sum(jnp.where(lane == idxs[kq], before, 0.0), axis=1, keepdims=True)
        meta = jnp.where(lane == kq, idxs[kq].astype(jnp.int32), meta)
        meta = jnp.where(lane == TOP_K + kq, rank.astype(jnp.int32), meta)
        gates = jnp.where(lane == kq, exps[kq] / denom, gates)
    meta_ref[...] = meta
    gate_ref[...] = gates


def _router(h, whi, wlo, bias, tm):
    t, d = h.shape
    return pl.pallas_call(
        _router_kernel,
        out_shape=(jax.ShapeDtypeStruct((t, LANES), jnp.int32),
                   jax.ShapeDtypeStruct((t, LANES), F32),
                   jax.ShapeDtypeStruct((1, LANES), F32)),
        grid=(t // tm,),
        in_specs=[pl.BlockSpec((tm, d), lambda i: (i, 0)),
                  pl.BlockSpec((d, LANES), lambda i: (0, 0)),
                  pl.BlockSpec((d, LANES), lambda i: (0, 0)),
                  pl.BlockSpec((1, LANES), lambda i: (0, 0))],
        out_specs=(pl.BlockSpec((tm, LANES), lambda i: (i, 0)),
                   pl.BlockSpec((tm, LANES), lambda i: (i, 0)),
                   pl.BlockSpec((1, LANES), lambda i: (0, 0))),
        scratch_shapes=[pltpu.VMEM((1, LANES), F32)],
        compiler_params=_cparams(("arbitrary",)),
        name="router",
    )(h, whi, wlo, bias)


def _expert_gather(rows_ref, h_hbm, xbuf, sem, slot, tm):
    for r in range(tm):
        pltpu.make_async_copy(h_hbm.at[rows_ref[0, 0, r]], xbuf.at[slot, r], sem.at[slot]).start()


def _expert_kernel(te_ref, nv_ref, rows_ref, rows_next_ref, rows_next2_ref, h_hbm, wgu_ref, bgu_ref,
                   wd_ref, bd_ref, y_ref, xbuf, sem, *, ff):
    g = pl.program_id(0)
    ng = pl.num_programs(0)
    tm = xbuf.shape[1]
    slot = lax.rem(g, 3)

    @pl.when(g == 0)
    def _():
        _expert_gather(rows_ref, h_hbm, xbuf, sem, 0, tm)

    @pl.when((g == 0) & (ng > 1))
    def _():
        _expert_gather(rows_next_ref, h_hbm, xbuf, sem, 1, tm)

    @pl.when(g + 2 < ng)
    def _():
        _expert_gather(rows_next2_ref, h_hbm, xbuf, sem, lax.rem(g + 2, 3), tm)

    pltpu.make_async_copy(h_hbm.at[pl.ds(0, tm)], xbuf.at[slot], sem.at[slot]).wait()

    @pl.when(g < nv_ref[0])
    def _():
        x = xbuf[slot].astype(BF16)
        gu = _dot(x, wgu_ref[0]) + bgu_ref[0]
        gt = jnp.minimum(gu[:, :ff], SWIGLU_LIMIT)
        up = jnp.clip(gu[:, ff:], -SWIGLU_LIMIT, SWIGLU_LIMIT)
        act = (up + 1.0) * (gt * _sigmoid(SWIGLU_ALPHA * gt))
        y_ref[...] = _dot(act.astype(BF16), wd_ref[0]) + bd_ref[0]

    @pl.when(g >= nv_ref[0])
    def _():
        y_ref[...] = jnp.zeros_like(y_ref)


def _experts(te, nv, rows3, h, wgu, bgu, wd, bd, tm):
    ntile = rows3.shape[0]
    t, d = h.shape
    e, _, f2 = wgu.shape
    ff = f2 // 2
    kern = functools.partial(_expert_kernel, ff=ff)
    grid_spec = pltpu.PrefetchScalarGridSpec(
        num_scalar_prefetch=2,
        grid=(ntile,),
        in_specs=[pl.BlockSpec((1, 1, tm), lambda g, te, nv: (g, 0, 0), memory_space=pltpu.SMEM),
                  pl.BlockSpec((1, 1, tm), lambda g, te, nv: (jnp.minimum(g + 1, ntile - 1), 0, 0),
                               memory_space=pltpu.SMEM),
                  pl.BlockSpec((1, 1, tm), lambda g, te, nv: (jnp.minimum(g + 2, ntile - 1), 0, 0),
                               memory_space=pltpu.SMEM),
                  pl.BlockSpec(memory_space=pl.ANY),
                  pl.BlockSpec((1, d, f2), lambda g, te, nv: (te[g], 0, 0)),
                  pl.BlockSpec((1, 1, f2), lambda g, te, nv: (te[g], 0, 0)),
                  pl.BlockSpec((1, ff, d), lambda g, te, nv: (te[g], 0, 0)),
                  pl.BlockSpec((1, 1, d), lambda g, te, nv: (te[g], 0, 0))],
        out_specs=pl.BlockSpec((tm, d), lambda g, te, nv: (g, 0)),
        scratch_shapes=[pltpu.VMEM((3, tm, d), F32), pltpu.SemaphoreType.DMA((3,))],
    )
    return pl.pallas_call(
        kern,
        out_shape=jax.ShapeDtypeStruct((ntile * tm, d), F32),
        grid_spec=grid_spec,
        compiler_params=_cparams(("arbitrary",)),
        name="experts",
    )(te, nv, rows3, rows3, rows3, h, wgu, bgu, wd, bd)


def _combine_gather(pos_ref, y_hbm, ybuf, sem, slot, tc):
    for r in range(tc):
        for kq in range(TOP_K):
            pltpu.make_async_copy(y_hbm.at[pos_ref[0, 0, r * TOP_K + kq]], ybuf.at[slot, kq, r],
                                  sem.at[slot]).start()


def _combine_kernel(pos_ref, pos_next_ref, y_hbm, gate_ref, h_ref, g_ref, b_ref, of_ref, ob_ref,
                    ybuf, sem, *, alpha):
    i = pl.program_id(0)
    n = pl.num_programs(0)
    tc = ybuf.shape[2]
    slot = lax.rem(i, 2)

    @pl.when(i == 0)
    def _():
        _combine_gather(pos_ref, y_hbm, ybuf, sem, 0, tc)

    @pl.when(i + 1 < n)
    def _():
        _combine_gather(pos_next_ref, y_hbm, ybuf, sem, 1 - slot, tc)

    for kq in range(TOP_K):
        pltpu.make_async_copy(y_hbm.at[pl.ds(0, tc)], ybuf.at[slot, kq], sem.at[slot]).wait()

    gates = gate_ref[...]
    f = gates[:, 0:1] * ybuf[slot, 0]
    for kq in range(1, TOP_K):
        f = f + gates[:, kq:kq + 1] * ybuf[slot, kq]
    hn = _layer_norm_rows(alpha * h_ref[...] + f, g_ref[...], b_ref[...])
    of_ref[...] = hn
    ob_ref[...] = hn.astype(ob_ref.dtype)


def _combine_ln(pos3, y, gates, h, g, b, tc, alpha):
    t, d = h.shape
    n = t // tc
    kern = functools.partial(_combine_kernel, alpha=alpha)
    return pl.pallas_call(
        kern,
        out_shape=(jax.ShapeDtypeStruct((t, d), F32), jax.ShapeDtypeStruct((t, d), BF16)),
        grid=(n,),
        in_specs=[pl.BlockSpec((1, 1, tc * TOP_K), lambda i: (i, 0, 0), memory_space=pltpu.SMEM),
                  pl.BlockSpec((1, 1, tc * TOP_K), lambda i: (jnp.minimum(i + 1, n - 1), 0, 0),
                               memory_space=pltpu.SMEM),
                  pl.BlockSpec(memory_space=pl.ANY),
                  pl.BlockSpec((tc, LANES), lambda i: (i, 0)),
                  pl.BlockSpec((tc, d), lambda i: (i, 0)),
                  pl.BlockSpec((1, d), lambda i: (0, 0)),
                  pl.BlockSpec((1, d), lambda i: (0, 0))],
        out_specs=(pl.BlockSpec((tc, d), lambda i: (i, 0)),
                   pl.BlockSpec((tc, d), lambda i: (i, 0))),
        scratch_shapes=[pltpu.VMEM((2, TOP_K, tc, d), F32), pltpu.SemaphoreType.DMA((2,))],
        compiler_params=_cparams(("arbitrary",)),
        name="combine_ln",
    )(pos3, pos3, y, gates, h, g, b)


def _config(x, p, w_in, b_forget, w_gla_gate, g_gla_norm, w_out, w_router, w_down, w_ple_proj):
    d = x.shape[-1]
    fh = b_forget.shape[1]
    rank = w_gla_gate.shape[1]
    gk = w_gla_gate.shape[2]
    dv = g_gla_norm.shape[1]
    mix = w_out.shape[1]
    width = w_in.shape[2]
    fw = (width - fh - 2 * gk - rank) - 2 * mix
    gv = mix - fw
    gh = gv // dv
    cfg = dict(d=d, fh=fh, rank=rank, gk=gk, dv=dv, mix=mix, fw=fw, gv=gv, gh=gh, dk=gk // gh,
               fdh=fw // fh, e=w_router.shape[2], ff=w_down.shape[2], ple=w_ple_proj.shape[1],
               depth=w_in.shape[0])
    assert cfg["fdh"] == LANES and fh + rank <= LANES and cfg["e"] <= LANES
    cfg["gq_blk"] = 3 * fw // LANES
    cfg["gk_blk"] = cfg["gq_blk"] + gk // LANES
    cfg["gv_blk"] = cfg["gk_blk"] + gk // LANES
    cfg["gr_blk"] = cfg["gv_blk"] + gv // LANES
    return cfg


def _prep_in_weights(w_in_l, b_forget_l, cfg):
    fw, fh, gk, gv, rank = cfg["fw"], cfg["fh"], cfg["gk"], cfg["gv"], cfg["rank"]
    o = 0
    parts = {}
    for name, size in (("fq", fw), ("fk", fw), ("fv", fw), ("ff", fh), ("gq", gk), ("gk", gk),
                       ("gv", gv), ("gr", gv), ("glr", rank)):
        parts[name] = w_in_l[:, o:o + size]
        o += size
    w_main = jnp.concatenate([parts[n] for n in ("fq", "fk", "fv", "gq", "gk", "gv", "gr")],
                             axis=1).astype(BF16)
    d = w_in_l.shape[0]
    w_aux = jnp.concatenate([parts["ff"], parts["glr"],
                             jnp.zeros((d, LANES - fh - rank), F32)], axis=1)
    aux_hi, aux_lo = _split_bf16(w_aux)
    bias = jnp.concatenate([b_forget_l, jnp.zeros((LANES - fh,), F32)])[None, :]
    return w_main, aux_hi, aux_lo, bias


def _routing_tables(meta, counts, cfg, t, tm_e, ntile):
    e = cfg["e"]
    topi = meta[:, :TOP_K]
    rank = meta[:, TOP_K:2 * TOP_K]
    cnt = counts[0, :e].astype(jnp.int32)
    padded = ((cnt + tm_e - 1) // tm_e) * tm_e
    ends = jnp.cumsum(padded)
    starts = ends - padded
    pos = starts[topi] + rank
    tok = jnp.broadcast_to(jnp.arange(t, dtype=jnp.int32)[:, None], (t, TOP_K))
    rows = jnp.zeros((ntile * tm_e,), jnp.int32).at[pos.reshape(-1)].set(tok.reshape(-1))
    tile_start = jnp.arange(ntile, dtype=jnp.int32) * tm_e
    te = jnp.sum((ends[None, :] <= tile_start[:, None]).astype(jnp.int32), axis=1)
    te = jnp.minimum(te, e - 1)
    nv = (ends[-1] // tm_e).astype(jnp.int32)[None]
    last = jnp.maximum(nv[0] - 1, 0)
    te = jnp.where(tile_start // tm_e < nv[0], te, te[last])
    return pos, rows, te, nv


def _tile(n, pref):
    return pref if n % pref == 0 else n


def kernel(x, p, w_in, b_forget, w_gla_gate, b_gla_gate, g_gla_norm, w_out, g_ln1, b_ln1, w_router,
           b_router, w_gate_up, b_gate_up, w_down, b_down, g_ln2, b_ln2, w_ple_gate, w_ple_proj,
           g_ln3, b_ln3):
    cfg = _config(x, p, w_in, b_forget, w_gla_gate, g_gla_norm, w_out, w_router, w_down, w_ple_proj)
    bsz, seq, d = x.shape
    assert bsz == 1
    t = seq
    depth = cfg["depth"]
    alpha = float((2 * depth) ** 0.25)
    e, gh, dk, dv, rank, fh = cfg["e"], cfg["gh"], cfg["dk"], cfg["dv"], cfg["rank"], cfg["fh"]

    tm_in = _tile(t, 2048)
    tn_in = 512
    tq = _tile(t, 4096)
    tk_fox = _tile(tq, 1024)
    tm_aux = tk_fox
    ts_fox = 512
    tm_ln = _tile(t, 512)
    tn_ln = _tile(d, 512)
    tm_r = _tile(t, 512)
    tm_e = 128
    tc = 128
    ntile = -(-(t * TOP_K + e * (tm_e - 1)) // tm_e)

    h = x.reshape(t, d)
    hb = h.astype(BF16)
    for l in range(depth):
        w_main, aux_hi, aux_lo, aux_bias = _prep_in_weights(w_in[l], b_forget[l], cfg)
        z3 = _inproj(hb, w_main, tm_in, tn_in)
        aux, cl, off = _aux_proj(hb, aux_hi, aux_lo, aux_bias, tm_aux)
        nkv = t // tk_fox
        cl3 = (-LOG2E * cl[:, :fh]).T.reshape(fh, nkv, tk_fox)
        off3 = jnp.broadcast_to((LOG2E * off[:, 0, :fh]).T[:, :, None], (fh, nkv, tk_fox))
        fox = _fox(z3, cl3, off3, fh, tq, tk_fox, ts_fox)

        wg = jnp.zeros((LANES, cfg["gk"]), F32).at[fh:fh + rank].set(w_gla_gate[l])
        wg = wg.reshape(LANES, gh, dk).transpose(1, 0, 2)
        wg_hi, wg_lo = _split_bf16(wg)
        bg = b_gla_gate[l].reshape(gh, 1, dk)
        gla3 = _gla(z3, aux, wg_hi, wg_lo, bg, g_gla_norm[l][None, :], cfg)

        h, hb = _outproj_ln(fox, gla3, w_out[l].astype(BF16), h, g_ln1[l][None, :],
                            b_ln1[l][None, :], tm_ln, tn_ln, alpha)

        wr = jnp.concatenate([w_router[l], jnp.zeros((d, LANES - e), F32)], axis=1)
        wr_hi, wr_lo = _split_bf16(wr)
        rb = jnp.concatenate([b_router[l], jnp.full((LANES - e,), NEG_BIG, F32)])[None, :]
        meta, gates, counts = _router(h, wr_hi, wr_lo, rb, tm_r)
        pos, rows, te, nv = _routing_tables(meta, counts, cfg, t, tm_e, ntile)
        y = _experts(te, nv, rows.reshape(ntile, 1, tm_e), h, w_gate_up[l].astype(BF16),
                     b_gate_up[l][:, None, :], w_down[l].astype(BF16), b_down[l][:, None, :], tm_e)
        h, hb = _combine_ln(pos.reshape(t // tc, 1, tc * TOP_K), y, gates, h, g_ln2[l][None, :],
                            b_ln2[l][None, :], tc, alpha)

        h, hb = _ple_ln(hb, w_ple_gate[l].astype(BF16), p[l].reshape(t, -1).astype(BF16),
                        w_ple_proj[l].astype(BF16), h, g_ln3[l][None, :], b_ln3[l][None, :],
                        tm_ln, tn_ln, alpha)
    return h.reshape(bsz, seq, d)
```
